```python
import math
import jax, jax.numpy as jnp
from jax import lax
import numpy as np

D_MODEL = 2048
BATCH = 2
SEQ = 4096
DEPTH = 2
DEC_BATCH = 32
DEC_SEQ = 32
PAST_LEN = 4096

CHUNK = 64
N_A_LAYERS = DEPTH // 2
N_B_LAYERS = DEPTH - N_A_LAYERS
GROUP_DIM = 16
N_GROUPS = D_MODEL // GROUP_DIM
STATE_DIM = 64
N_HEADS = 8
HEAD_DIM = 128
QK_WIDTH = N_HEADS * 2 * HEAD_DIM
V_HEAD_DIM = 2 * HEAD_DIM
V_WIDTH = N_HEADS * V_HEAD_DIM
D_FF = -(-8 * D_MODEL // (3 * 256)) * 256
NUM_BUCKETS = 32
MAX_DISTANCE = 128
Q_BLOCK = 128
EPS = 1e-6
NEG_INF = -1e30

kernel_name = "yoco_s5_diffattn_streaming_step"


def rmsnorm(x, g):
    xf = x.astype(jnp.float32)
    y = xf * lax.rsqrt(jnp.mean(xf * xf, axis=-1, keepdims=True) + EPS) * g.astype(jnp.float32)
    return y.astype(x.dtype)


def rel_bucket(rel):
    half = NUM_BUCKETS // 2
    max_exact = half // 2
    n = -rel
    ret = jnp.where(n < 0, half, 0)
    n = jnp.abs(n)
    nf = jnp.maximum(n, 1).astype(jnp.float32)
    large = max_exact + (jnp.log(nf / max_exact) / math.log(MAX_DISTANCE / max_exact)
                         * (half - max_exact)).astype(jnp.int32)
    large = jnp.minimum(large, half - 1)
    return ret + jnp.where(n < max_exact, n, large)


def s5_ssm(u, h0_re, h0_im, lam_re, lam_im, log_dt, b_re, b_im, c_re, c_im, d):
    f32 = jnp.float32
    bsz, seq_len, _ = u.shape
    ug = u.astype(f32).reshape(bsz, seq_len, N_GROUPS, GROUP_DIM)
    lam_re = lam_re.astype(f32); lam_im = lam_im.astype(f32)
    dt = jnp.exp(log_dt.astype(f32))[:, None]
    mag = jnp.exp(lam_re * dt); ang = lam_im * dt
    a_re = mag * jnp.cos(ang); a_im = mag * jnp.sin(ang)
    den = lam_re * lam_re + lam_im * lam_im
    n_re = a_re - 1.0; n_im = a_im
    f_re = (n_re * lam_re + n_im * lam_im) / den
    f_im = (n_im * lam_re - n_re * lam_im) / den
    b_re = b_re.astype(f32); b_im = b_im.astype(f32)
    bb_re = f_re[..., None] * b_re - f_im[..., None] * b_im
    bb_im = f_re[..., None] * b_im + f_im[..., None] * b_re
    bu_re = jnp.einsum('blgh,gph->blgp', ug, bb_re)
    bu_im = jnp.einsum('blgh,gph->blgp', ug, bb_im)
    h0_re = h0_re.astype(f32); h0_im = h0_im.astype(f32)
    bu_re = bu_re.at[:, 0].add(a_re * h0_re - a_im * h0_im)
    bu_im = bu_im.at[:, 0].add(a_re * h0_im + a_im * h0_re)
    ar = jnp.broadcast_to(a_re, bu_re.shape)
    ai = jnp.broadcast_to(a_im, bu_re.shape)

    def combine(e1, e2):
        a1r, a1i, b1r, b1i = e1
        a2r, a2i, b2r, b2i = e2
        return (a2r * a1r - a2i * a1i,
                a2r * a1i + a2i * a1r,
                a2r * b1r - a2i * b1i + b2r,
                a2r * b1i + a2i * b1r + b2i)

    _, _, hr, hi = lax.associative_scan(combine, (ar, ai, bu_re, bu_im), axis=1)
    y = (jnp.einsum('blgp,ghp->blgh', hr, c_re.astype(f32))
         - jnp.einsum('blgp,ghp->blgh', hi, c_im.astype(f32))
         + d.astype(f32).reshape(N_GROUPS, GROUP_DIM) * ug)
    return y.reshape(bsz, seq_len, D_MODEL), hr[:, -1], hi[:, -1]


def diff_attn_block(q, k, v, q_pos, k_pos, rel_bias, lam):
    logits = jnp.einsum('bqhmd,bkhmd->bhmqk', q, k).astype(jnp.float32)
    bucket = rel_bucket(k_pos[None, :] - q_pos[:, None])
    bias = jnp.transpose(rel_bias.astype(jnp.float32)[bucket], (2, 3, 0, 1))
    mask = (k_pos[None, :] // CHUNK) <= (q_pos[:, None] // CHUNK)
    logits = jnp.where(mask, logits + bias, NEG_INF)
    p = jax.nn.softmax(logits, axis=-1)
    w = p[:, :, 0] - lam * p[:, :, 1]
    return jnp.einsum('bhqk,bkhe->bqhe', w.astype(v.dtype), v)


def attend(q, k, v, q_pos, k_pos, rel_bias, lam):
    bsz, lq = q.shape[0], q.shape[1]
    if lq <= Q_BLOCK or lq % Q_BLOCK != 0:
        return diff_attn_block(q, k, v, q_pos, k_pos, rel_bias, lam)
    nb = lq // Q_BLOCK
    qb = jnp.swapaxes(q.reshape(bsz, nb, Q_BLOCK, N_HEADS, 2, HEAD_DIM), 0, 1)
    pb = q_pos.reshape(nb, Q_BLOCK)
    out = lax.map(lambda a: diff_attn_block(a[0], k, v, a[1], k_pos, rel_bias, lam), (qb, pb))
    return jnp.swapaxes(out, 0, 1).reshape(bsz, lq, N_HEADS, V_HEAD_DIM)


def swiglu(x, w_in, w_out):
    gu = x @ w_in
    return (jax.nn.silu(gu[..., :D_FF]) * gu[..., D_FF:]) @ w_out


def run_trunk(x, h0_re, h0_im, past_k, past_v, p):
    bsz, seq_len, _ = x.shape
    p0 = past_k.shape[1]
    q_pos = p0 + jnp.arange(seq_len, dtype=jnp.int32)
    k_pos = jnp.arange(p0 + seq_len, dtype=jnp.int32)
    h = x
    ssm_re_out, ssm_im_out = [], []
    k_new = v_new = k_all = v_all = None
    for layer in range(DEPTH):
        u = rmsnorm(h, p['mix_norm_g'][layer])
        if layer < N_A_LAYERS:
            y, hr, hi = s5_ssm(u, h0_re[layer], h0_im[layer],
                               p['ssm_lambda_re'][layer], p['ssm_lambda_im'][layer], p['ssm_log_dt'][layer],
                               p['ssm_b_re'][layer], p['ssm_b_im'][layer],
                               p['ssm_c_re'][layer], p['ssm_c_im'][layer], p['ssm_d'][layer])
            ssm_re_out.append(hr)
            ssm_im_out.append(hi)
            g = jax.nn.gelu(y).astype(h.dtype)
            ab = g @ p['w_glu'][layer]
            h = h + ab[..., :D_MODEL] * jax.nn.sigmoid(ab[..., D_MODEL:])
        else:
            if layer == N_A_LAYERS:
                kv = rmsnorm(h, p['kv_norm_g']) @ p['w_kv']
                k_new = rmsnorm(kv[..., :QK_WIDTH].reshape(bsz, seq_len, N_HEADS, 2, HEAD_DIM), p['k_norm_g'])
                v_new = kv[..., QK_WIDTH:].reshape(bsz, seq_len, N_HEADS, V_HEAD_DIM)
                k_all = jnp.concatenate([past_k.astype(k_new.dtype), k_new], axis=1)
                v_all = jnp.concatenate([past_v.astype(v_new.dtype), v_new], axis=1)
            j = layer - N_A_LAYERS
            lam_init = 0.8 - 0.6 * math.exp(-0.3 * layer)
            lp = p['diff_lambda'][j].astype(jnp.float32)
            lam = jnp.exp(jnp.sum(lp[0] * lp[1])) - jnp.exp(jnp.sum(lp[2] * lp[3])) + lam_init
            q = (u @ p['w_q'][j]).reshape(bsz, seq_len, N_HEADS, 2, HEAD_DIM)
            q = rmsnorm(q, p['q_norm_g'][j]) * (HEAD_DIM ** -0.5)
            o = attend(q, k_all, v_all, q_pos, k_pos, p['rel_bias'], lam)
            o = rmsnorm(o, p['subln_g'][j]) * (1.0 - lam_init)
            h = h + o.reshape(bsz, seq_len, V_WIDTH) @ p['w_o'][j]
        h = h + swiglu(rmsnorm(h, p['ffn_norm_g'][layer]), p['w_ffn_in'][layer], p['w_ffn_out'][layer])
    return h, jnp.stack(ssm_re_out), jnp.stack(ssm_im_out), k_new, v_new


def setup_inputs(seed: int = 0) -> dict:
    key = jax.random.key(seed)
    ks = iter(jax.random.split(key, 40))

    def nrm(shape, scale):
        return scale * jax.random.normal(next(ks), shape, jnp.float32)

    n_idx = jnp.arange(STATE_DIM, dtype=jnp.float32)
    inputs = {}
    inputs['x_prompt'] = nrm((BATCH, SEQ, D_MODEL), 1.0)
    inputs['x_sample'] = nrm((DEC_BATCH, DEC_SEQ, D_MODEL), 1.0)
    inputs['state_ssm_re'] = nrm((N_A_LAYERS, DEC_BATCH, N_GROUPS, STATE_DIM), 0.5)
    inputs['state_ssm_im'] = nrm((N_A_LAYERS, DEC_BATCH, N_GROUPS, STATE_DIM), 0.5)
    inputs['cache_k'] = nrm((DEC_BATCH, PAST_LEN, N_HEADS, 2, HEAD_DIM), 1.0)
    inputs['cache_v'] = nrm((DEC_BATCH, PAST_LEN, N_HEADS, V_HEAD_DIM), 1.0)
    inputs['mix_norm_g'] = 1.0 + nrm((DEPTH, D_MODEL), 0.02)
    inputs['ffn_norm_g'] = 1.0 + nrm((DEPTH, D_MODEL), 0.02)
    inputs['ssm_lambda_re'] = -0.5 + nrm((N_A_LAYERS, N_GROUPS, STATE_DIM), 0.01)
    inputs['ssm_lambda_im'] = jnp.pi * n_idx + nrm((N_A_LAYERS, N_GROUPS, STATE_DIM), 0.01)
    inputs['ssm_log_dt'] = jax.random.uniform(next(ks), (N_A_LAYERS, N_GROUPS), jnp.float32,
                                              minval=math.log(1e-3), maxval=math.log(1e-1))
    inputs['ssm_b_re'] = nrm((N_A_LAYERS, N_GROUPS, STATE_DIM, GROUP_DIM), (2 * GROUP_DIM) ** -0.5)
    inputs['ssm_b_im'] = nrm((N_A_LAYERS, N_GROUPS, STATE_DIM, GROUP_DIM), (2 * GROUP_DIM) ** -0.5)
    inputs['ssm_c_re'] = nrm((N_A_LAYERS, N_GROUPS, GROUP_DIM, STATE_DIM), STATE_DIM ** -0.5)
    inputs['ssm_c_im'] = nrm((N_A_LAYERS, N_GROUPS, GROUP_DIM, STATE_DIM), STATE_DIM ** -0.5)
    inputs['ssm_d'] = nrm((N_A_LAYERS, D_MODEL), 1.0)
    inputs['w_glu'] = nrm((N_A_LAYERS, D_MODEL, 2 * D_MODEL), D_MODEL ** -0.5)
    inputs['kv_norm_g'] = 1.0 + nrm((D_MODEL,), 0.02)
    inputs['w_kv'] = nrm((D_MODEL, QK_WIDTH + V_WIDTH), D_MODEL ** -0.5)
    inputs['k_norm_g'] = 1.0 + nrm((HEAD_DIM,), 0.02)
    inputs['w_q'] = nrm((N_B_LAYERS, D_MODEL, QK_WIDTH), D_MODEL ** -0.5)
    inputs['q_norm_g'] = 1.0 + nrm((N_B_LAYERS, HEAD_DIM), 0.02)
    inputs['diff_lambda'] = nrm((N_B_LAYERS, 4, HEAD_DIM), 0.1)
    inputs['subln_g'] = 1.0 + nrm((N_B_LAYERS, V_HEAD_DIM), 0.02)
    inputs['w_o'] = nrm((N_B_LAYERS, V_WIDTH, D_MODEL), V_WIDTH ** -0.5)
    inputs['rel_bias'] = nrm((NUM_BUCKETS, N_HEADS, 2), 0.5)
    inputs['w_ffn_in'] = nrm((DEPTH, D_MODEL, 2 * D_FF), D_MODEL ** -0.5)
    inputs['w_ffn_out'] = nrm((DEPTH, D_FF, D_MODEL), D_FF ** -0.5)
    return inputs


def reference(x_prompt, x_sample, state_ssm_re, state_ssm_im, cache_k, cache_v,
              mix_norm_g, ffn_norm_g, ssm_lambda_re, ssm_lambda_im, ssm_log_dt,
              ssm_b_re, ssm_b_im, ssm_c_re, ssm_c_im, ssm_d, w_glu,
              kv_norm_g, w_kv, k_norm_g, w_q, q_norm_g, diff_lambda, subln_g, w_o,
              rel_bias, w_ffn_in, w_ffn_out):
    params = dict(mix_norm_g=mix_norm_g, ffn_norm_g=ffn_norm_g,
                  ssm_lambda_re=ssm_lambda_re, ssm_lambda_im=ssm_lambda_im, ssm_log_dt=ssm_log_dt,
                  ssm_b_re=ssm_b_re, ssm_b_im=ssm_b_im, ssm_c_re=ssm_c_re, ssm_c_im=ssm_c_im,
                  ssm_d=ssm_d, w_glu=w_glu, kv_norm_g=kv_norm_g, w_kv=w_kv, k_norm_g=k_norm_g,
                  w_q=w_q, q_norm_g=q_norm_g, diff_lambda=diff_lambda, subln_g=subln_g, w_o=w_o,
                  rel_bias=rel_bias, w_ffn_in=w_ffn_in, w_ffn_out=w_ffn_out)
    bp = x_prompt.shape[0]
    zero_h = jnp.zeros((N_A_LAYERS, bp, N_GROUPS, STATE_DIM), jnp.float32)
    no_k = jnp.zeros((bp, 0, N_HEADS, 2, HEAD_DIM), cache_k.dtype)
    no_v = jnp.zeros((bp, 0, N_HEADS, V_HEAD_DIM), cache_v.dtype)
    y_prompt, ssm_re_p, ssm_im_p, k_p, v_p = run_trunk(x_prompt, zero_h, zero_h, no_k, no_v, params)
    y_sample, ssm_re_s, ssm_im_s, k_s, v_s = run_trunk(x_sample, state_ssm_re, state_ssm_im,
                                                       cache_k, cache_v, params)
    return (y_prompt, y_sample, ssm_re_p, ssm_im_p, k_p, v_p, ssm_re_s, ssm_im_s, k_s, v_s)
```

```python
import functools
import math

import jax
import jax.numpy as jnp
from jax import lax
from jax.experimental import pallas as pl
from jax.experimental.pallas import tpu as pltpu

F32 = jnp.float32
BF16 = jnp.bfloat16

EPS = 1e-6
NEG_INF = -1e30
CHUNK = 64
NUM_BUCKETS = 32
MAX_DISTANCE = 128

V7X_VMEM_BYTES = 64 * 1024 * 1024
LANES = 128
SUBLANES = 8

SSM_GROUPS_PER_BLOCK = 16
SSM_TIME_CHUNK = 16
SSM_STREAMS = 32
ROW_TILE = 512
COL_TILE = 512
ATTN_TILE = 256
DEC_KEY_TILE = 512


def _cparams(semantics, vmem_mb):
    return pltpu.CompilerParams(dimension_semantics=semantics,
                                vmem_limit_bytes=min(vmem_mb * 1024 * 1024, V7X_VMEM_BYTES))


def _rmsnorm(x, g):
    ms = jnp.mean(x * x, axis=-1, keepdims=True)
    return x * lax.rsqrt(ms + EPS) * g


def _dot(a, b):
    return jnp.dot(a, b, preferred_element_type=F32)


def _dot_nt(a, b):
    return lax.dot_general(a, b, (((1,), (1,)), ((), ())), preferred_element_type=F32)


def _norm_kernel(x_ref, g_ref, o_ref):
    o_ref[...] = _rmsnorm(x_ref[...], g_ref[...]).astype(o_ref.dtype)


def rmsnorm_rows(x, g, out_dtype=F32):
    n, d = x.shape
    tm = min(ROW_TILE, n)
    return pl.pallas_call(
        _norm_kernel,
        grid=(n // tm,),
        in_specs=[pl.BlockSpec((tm, d), lambda i: (i, 0)),
                  pl.BlockSpec((1, d), lambda i: (0, 0))],
        out_specs=pl.BlockSpec((tm, d), lambda i: (i, 0)),
        out_shape=jax.ShapeDtypeStruct((n, d), out_dtype),
        compiler_params=_cparams(("parallel",), 32),
        name="rmsnorm_rows",
    )(x, g.reshape(1, d))


def _ssm_kernel(*refs, streams, steps, emit_y):
    if emit_y:
        (u_ref, h0r_ref, h0i_ref, ar_ref, ai_ref, bbr_ref, bbi_ref, ccr_ref, cci_ref, d_ref,
         g_ref, her_ref, hei_ref, bur, bui) = refs
    else:
        (u_ref, h0r_ref, h0i_ref, ar_ref, ai_ref, bbr_ref, bbi_ref,
         her_ref, hei_ref, bur, bui) = refs

    @pl.when(pl.program_id(1) == 0)
    def _():
        her_ref[...] = h0r_ref[...]
        hei_ref[...] = h0i_ref[...]

    u = u_ref[...]
    ub = u.astype(BF16)
    bur[...] = _dot(ub, bbr_ref[...])
    bui[...] = _dot(ub, bbi_ref[...])

    n_state = bur.shape[1]
    for c in range(n_state // LANES):
        cols = slice(c * LANES, (c + 1) * LANES)
        ar = jnp.broadcast_to(ar_ref[:, cols], (streams, LANES))
        ai = jnp.broadcast_to(ai_ref[:, cols], (streams, LANES))
        hr = her_ref[:, cols]
        hi = hei_ref[:, cols]
        for t in range(steps):
            rows = slice(t * streams, (t + 1) * streams)
            nr = ar * hr - ai * hi + bur[rows, cols]
            ni = ar * hi + ai * hr + bui[rows, cols]
            bur[rows, cols] = nr
            bui[rows, cols] = ni
            hr, hi = nr, ni
        her_ref[:, cols] = hr
        hei_ref[:, cols] = hi

    if emit_y:
        y = (_dot(bur[...].astype(BF16), ccr_ref[...]) + _dot(bui[...].astype(BF16), cci_ref[...])
             + d_ref[...] * u)
        g_ref[...] = jax.nn.gelu(y).astype(g_ref.dtype)


def ssm_scan(u, h0r, h0i, prm, emit_y):
    n, d = u.shape
    s = h0r.shape[0]
    steps = SSM_TIME_CHUNK
    rows = steps * s
    nb, cb, sb = prm["bbr"].shape
    grid = (nb, n // rows)
    in_specs = [
        pl.BlockSpec((rows, cb), lambda b, t: (t, b)),
        pl.BlockSpec((s, sb), lambda b, t: (0, b)),
        pl.BlockSpec((s, sb), lambda b, t: (0, b)),
        pl.BlockSpec((1, sb), lambda b, t: (0, b)),
        pl.BlockSpec((1, sb), lambda b, t: (0, b)),
        pl.BlockSpec((None, cb, sb), lambda b, t: (b, 0, 0)),
        pl.BlockSpec((None, cb, sb), lambda b, t: (b, 0, 0)),
    ]
    args = [u, h0r, h0i, prm["a_re"], prm["a_im"], prm["bbr"], prm["bbi"]]
    state_spec = pl.BlockSpec((s, sb), lambda b, t: (0, b))
    state_shape = jax.ShapeDtypeStruct(h0r.shape, F32)
    if emit_y:
        in_specs += [
            pl.BlockSpec((None, sb, cb), lambda b, t: (b, 0, 0)),
            pl.BlockSpec((None, sb, cb), lambda b, t: (b, 0, 0)),
            pl.BlockSpec((1, cb), lambda b, t: (0, b)),
        ]
        args += [prm["ccr"], prm["cci"], prm["d"]]
        out_specs = [pl.BlockSpec((rows, cb), lambda b, t: (t, b)), state_spec, state_spec]
        out_shape = [jax.ShapeDtypeStruct((n, d), BF16), state_shape, state_shape]
    else:
        out_specs = [state_spec, state_spec]
        out_shape = [state_shape, state_shape]
    outs = pl.pallas_call(
        functools.partial(_ssm_kernel, streams=s, steps=steps, emit_y=emit_y),
        grid=grid,
        in_specs=in_specs,
        out_specs=out_specs,
        out_shape=out_shape,
        scratch_shapes=[pltpu.VMEM((rows, sb), F32), pltpu.VMEM((rows, sb), F32)],
        compiler_params=_cparams(("parallel", "arbitrary"), 40),
        name="ssm_scan_y" if emit_y else "ssm_scan_state",
    )(*args)
    if emit_y:
        return outs[0], outs[1], outs[2]
    return None, outs[0], outs[1]


def _seg_combine_kernel(er_ref, ei_ref, ar_ref, ai_ref, hr_ref, hi_ref, *, n_batch, n_seg, seg_len):
    pr, pi = ar_ref[...], ai_ref[...]
    tr, ti = None, None
    e = seg_len
    while e:
        if e & 1:
            if tr is None:
                tr, ti = pr, pi
            else:
                tr, ti = tr * pr - ti * pi, tr * pi + ti * pr
        e >>= 1
        if e:
            pr, pi = pr * pr - pi * pi, 2.0 * pr * pi
    for b in range(n_batch):
        hr = jnp.zeros_like(tr)
        hi = jnp.zeros_like(ti)
        for s in range(n_seg):
            row = slice(b * n_seg + s, b * n_seg + s + 1)
            hr_ref[row, :] = hr
            hi_ref[row, :] = hi
            hr, hi = tr * hr - ti * hi + er_ref[row, :], tr * hi + ti * hr + ei_ref[row, :]


def seg_combine(er, ei, a_re, a_im, n_batch, n_seg, seg_len):
    s, n = er.shape
    tn = min(1024, n)
    spec = pl.BlockSpec((s, tn), lambda j: (0, j))
    aspec = pl.BlockSpec((1, tn), lambda j: (0, j))
    return pl.pallas_call(
        functools.partial(_seg_combine_kernel, n_batch=n_batch, n_seg=n_seg, seg_len=seg_len),
        grid=(n // tn,),
        in_specs=[spec, spec, aspec, aspec],
        out_specs=[spec, spec],
        out_shape=[jax.ShapeDtypeStruct((s, n), F32)] * 2,
        compiler_params=_cparams(("parallel",), 32),
        name="ssm_seg_combine",
    )(er, ei, a_re, a_im)


def _glu_kernel(g_ref, wa_ref, wb_ref, x_ref, o_ref):
    g = g_ref[...]
    o_ref[...] = x_ref[...] + _dot(g, wa_ref[...]) * jax.nn.sigmoid(_dot(g, wb_ref[...]))


def glu_residual(g, w, x):
    n, k = g.shape
    d = x.shape[1]
    tm, tn = min(ROW_TILE, n), min(COL_TILE, d)
    nj = d // tn
    return pl.pallas_call(
        _glu_kernel,
        grid=(n // tm, nj),
        in_specs=[pl.BlockSpec((tm, k), lambda i, j: (i, 0)),
                  pl.BlockSpec((k, tn), lambda i, j: (0, j)),
                  pl.BlockSpec((k, tn), lambda i, j: (0, j + nj)),
                  pl.BlockSpec((tm, tn), lambda i, j: (i, j))],
        out_specs=pl.BlockSpec((tm, tn), lambda i, j: (i, j)),
        out_shape=jax.ShapeDtypeStruct((n, d), F32),
        compiler_params=_cparams(("parallel", "arbitrary"), 40),
        name="glu_residual",
    )(g, w, w, x)


def _mm_res_kernel(a_ref, w_ref, x_ref, o_ref):
    o_ref[...] = x_ref[...] + _dot(a_ref[...], w_ref[...])


def matmul_residual(a, w, x):
    n, k = a.shape
    d = w.shape[1]
    tm, tn = min(ROW_TILE, n), min(COL_TILE, d)
    return pl.pallas_call(
        _mm_res_kernel,
        grid=(n // tm, d // tn),
        in_specs=[pl.BlockSpec((tm, k), lambda i, j: (i, 0)),
                  pl.BlockSpec((k, tn), lambda i, j: (0, j)),
                  pl.BlockSpec((tm, tn), lambda i, j: (i, j))],
        out_specs=pl.BlockSpec((tm, tn), lambda i, j: (i, j)),
        out_shape=jax.ShapeDtypeStruct((n, d), F32),
        compiler_params=_cparams(("parallel", "arbitrary"), 40),
        name="matmul_residual",
    )(a, w, x)


def _ffn_kernel(x_ref, gn_ref, wg_ref, wu_ref, wo_ref, o_ref, xn_ref):
    @pl.when(pl.program_id(1) == 0)
    def _():
        x = x_ref[...]
        xn_ref[...] = _rmsnorm(x, gn_ref[...]).astype(BF16)
        o_ref[...] = x

    xn = xn_ref[...]
    hid = (jax.nn.silu(_dot(xn, wg_ref[...])) * _dot(xn, wu_ref[...])).astype(BF16)
    o_ref[...] += _dot(hid, wo_ref[...])


def ffn_residual(x, gn, w_in, w_out):
    n, d = x.shape
    f = w_out.shape[0]
    tm = min(ROW_TILE, n)
    tf = COL_TILE if f % COL_TILE == 0 else 256
    nf = f // tf
    return pl.pallas_call(
        _ffn_kernel,
        grid=(n // tm, nf),
        in_specs=[pl.BlockSpec((tm, d), lambda i, j: (i, 0)),
                  pl.BlockSpec((1, d), lambda i, j: (0, 0)),
                  pl.BlockSpec((d, tf), lambda i, j: (0, j)),
                  pl.BlockSpec((d, tf), lambda i, j: (0, j + nf)),
                  pl.BlockSpec((tf, d), lambda i, j: (j, 0))],
        out_specs=pl.BlockSpec((tm, d), lambda i, j: (i, 0)),
        out_shape=jax.ShapeDtypeStruct((n, d), F32),
        scratch_shapes=[pltpu.VMEM((tm, d), BF16)],
        compiler_params=_cparams(("parallel", "arbitrary"), 48),
        name="ffn_residual",
    )(x, gn.reshape(1, d), w_in, w_in, w_out)


def _proj_kernel(x_ref, gn_ref, w_ref, hg_ref, o_ref, xn_ref, *, n_norm_blocks, head_dim, scale):
    j = pl.program_id(1)

    @pl.when(j == 0)
    def _():
        xn_ref[...] = _rmsnorm(x_ref[...], gn_ref[...]).astype(BF16)

    acc = _dot(xn_ref[...], w_ref[...])
    tn = acc.shape[1]

    @pl.when(j < n_norm_blocks)
    def _():
        for c in range(tn // head_dim):
            cols = slice(c * head_dim, (c + 1) * head_dim)
            y = _rmsnorm(acc[:, cols], hg_ref[...])
            if scale != 1.0:
                y = y * scale
            o_ref[:, cols] = y.astype(o_ref.dtype)

    @pl.when(j >= n_norm_blocks)
    def _():
        o_ref[...] = acc.astype(o_ref.dtype)


def norm_proj_headnorm(x, gn, w, hg, n_norm_cols, scale, out_dtype):
    n, d = x.shape
    m = w.shape[1]
    hd = hg.shape[0]
    tm, tn = min(ROW_TILE, n), min(COL_TILE, m)
    assert n_norm_cols % tn == 0 and tn % hd == 0
    return pl.pallas_call(
        functools.partial(_proj_kernel, n_norm_blocks=n_norm_cols // tn, head_dim=hd, scale=scale),
        grid=(n // tm, m // tn),
        in_specs=[pl.BlockSpec((tm, d), lambda i, j: (i, 0)),
                  pl.BlockSpec((1, d), lambda i, j: (0, 0)),
                  pl.BlockSpec((d, tn), lambda i, j: (0, j)),
                  pl.BlockSpec((1, hd), lambda i, j: (0, 0))],
        out_specs=pl.BlockSpec((tm, tn), lambda i, j: (i, j)),
        out_shape=jax.ShapeDtypeStruct((n, m), out_dtype),
        scratch_shapes=[pltpu.VMEM((tm, d), BF16)],
        compiler_params=_cparams(("parallel", "arbitrary"), 40),
        name="norm_proj_headnorm",
    )(x, gn.reshape(1, d), w, hg.reshape(1, hd))


def _bucket_thresholds():
    half = NUM_BUCKETS // 2
    max_exact = half // 2
    steps = half - max_exact
    thr = []
    for j in range(1, steps):
        n = max_exact
        while (n ** steps) * (max_exact ** j) < (max_exact ** steps) * (MAX_DISTANCE ** j):
            n += 1
        thr.append(n)
    return max_exact, thr


def _bias_kernel(rb_ref, o_ref, *, n_hm, q0, k_start, k_stride):
    hm = pl.program_id(0)
    tq, tk = o_ref.shape
    k0 = k_start + pl.program_id(1) * k_stride
    qpos = q0 + lax.broadcasted_iota(jnp.int32, (tq, tk), 0)
    kpos = k0 + lax.broadcasted_iota(jnp.int32, (tq, tk), 1)
    n = qpos - kpos
    na = jnp.abs(n)
    max_exact, thr = _bucket_thresholds()
    large = jnp.full((tq, tk), max_exact, jnp.int32)
    for t in thr:
        large = large + jnp.where(na >= t, 1, 0)
    bucket = jnp.where(n < 0, NUM_BUCKETS // 2, 0) + jnp.where(na < max_exact, na, large)
    val = jnp.zeros((tq, tk), F32)
    for b in range(NUM_BUCKETS):
        val = jnp.where(bucket == b, rb_ref[b * n_hm + hm], val)
    shift = CHUNK.bit_length() - 1
    visible = lax.shift_right_logical(kpos, shift) <= lax.shift_right_logical(qpos, shift)
    o_ref[...] = jnp.where(visible, val, NEG_INF)


def bias_tiles(rel_bias, tq, tk, n_tiles, q0, k_start, k_stride):
    n_hm = rel_bias.shape[1] * rel_bias.shape[2]
    return pl.pallas_call(
        functools.partial(_bias_kernel, n_hm=n_hm, q0=q0, k_start=k_start, k_stride=k_stride),
        grid=(n_hm, n_tiles),
        in_specs=[pl.BlockSpec(memory_space=pltpu.SMEM)],
        out_specs=pl.BlockSpec((None, None, tq, tk), lambda a, t: (a, t, 0, 0)),
        out_shape=jax.ShapeDtypeStruct((n_hm, n_tiles, tq, tk), F32),
        compiler_params=_cparams(("parallel", "parallel"), 32),
        name="rel_bias_tiles",
    )(rel_bias.reshape(-1))


def _diff_lambda(dl_ref, lam_init):
    dl = dl_ref[...]
    return (jnp.exp(jnp.sum(dl[0:1] * dl[1:2], axis=-1, keepdims=True))
            - jnp.exp(jnp.sum(dl[2:3] * dl[3:4], axis=-1, keepdims=True)) + lam_init)


def _online_softmax_step(s, v, m_ref, l_ref, acc_ref, idx):
    m_prev = m_ref[idx]
    m_new = jnp.maximum(m_prev, jnp.max(s, axis=-1, keepdims=True))
    alpha = jnp.exp(m_prev - m_new)
    p = jnp.exp(s - m_new)
    l_ref[idx] = alpha * l_ref[idx] + jnp.sum(p, axis=-1, keepdims=True)
    acc_ref[idx] = alpha * acc_ref[idx] + _dot(p.astype(BF16), v)
    m_ref[idx] = m_new


def _attn_prompt_kernel(dl_ref, sg_ref, q_ref, k_ref, v_ref, b_ref, o_ref, m_sc, l_sc, acc_sc,
                        *, head_dim, lam_init):
    i, j = pl.program_id(2), pl.program_id(3)

    @pl.when(j == 0)
    def _():
        m_sc[...] = jnp.full(m_sc.shape, NEG_INF, F32)
        l_sc[...] = jnp.zeros(l_sc.shape, F32)
        acc_sc[...] = jnp.zeros(acc_sc.shape, F32)

    @pl.when(j <= i)
    def _():
        k = k_ref[...].astype(BF16)
        v = v_ref[...].astype(BF16)
        for m in range(2):
            cols = slice(m * head_dim, (m + 1) * head_dim)
            s = _dot_nt(q_ref[:, cols], k[:, cols]) + b_ref[m]
            _online_softmax_step(s, v, m_sc, l_sc, acc_sc, m)

    @pl.when(j == i)
    def _():
        lam = _diff_lambda(dl_ref, lam_init)
        o = acc_sc[0] / l_sc[0] - lam * (acc_sc[1] / l_sc[1])
        o_ref[...] = (_rmsnorm(o, sg_ref[...]) * (1.0 - lam_init)).astype(o_ref.dtype)


def attn_prompt(q, k, v, bias, dl, sg, n_heads, head_dim, lam_init):
    bsz, seq, width = q.shape
    t = bias.shape[-1]
    nq = seq // t
    hw = 2 * head_dim
    return pl.pallas_call(
        functools.partial(_attn_prompt_kernel, head_dim=head_dim, lam_init=lam_init),
        grid=(bsz, n_heads, nq, nq),
        in_specs=[pl.BlockSpec((4, head_dim), lambda b, h, i, j: (0, 0)),
                  pl.BlockSpec((1, hw), lambda b, h, i, j: (0, 0)),
                  pl.BlockSpec((None, t, hw), lambda b, h, i, j: (b, i, h)),
                  pl.BlockSpec((None, t, hw), lambda b, h, i, j: (b, jnp.minimum(j, i), h)),
                  pl.BlockSpec((None, t, hw), lambda b, h, i, j: (b, jnp.minimum(j, i), h)),
                  pl.BlockSpec((2, None, t, t), lambda b, h, i, j: (h, jnp.clip(i - j, 0, 2), 0, 0))],
        out_specs=pl.BlockSpec((None, t, hw), lambda b, h, i, j: (b, i, h)),
        out_shape=jax.ShapeDtypeStruct((bsz, seq, width), BF16),
        scratch_shapes=[pltpu.VMEM((2, t, 1), F32), pltpu.VMEM((2, t, 1), F32),
                        pltpu.VMEM((2, t, hw), F32)],
        compiler_params=_cparams(("parallel", "parallel", "parallel", "arbitrary"), 40),
        name="attn_prompt",
    )(dl, sg.reshape(1, hw), q, k, v, bias)


def _attn_decode_kernel(dl_ref, sg_ref, q_ref, kn_ref, vn_ref, kc_ref, vc_ref, bn_ref, bc_ref, o_ref,
                        m_sc, l_sc, acc_sc, *, n_heads, head_dim, lam_init):
    j = pl.program_id(1)
    hw = 2 * head_dim

    def attend(k, v, b_ref):
        for h in range(n_heads):
            vh = v[:, h * hw:(h + 1) * hw]
            for m in range(2):
                idx = 2 * h + m
                cols = slice(idx * head_dim, (idx + 1) * head_dim)
                s = _dot_nt(q_ref[:, cols], k[:, cols]) + b_ref[idx]
                _online_softmax_step(s, vh, m_sc, l_sc, acc_sc, idx)

    @pl.when(j == 0)
    def _():
        m_sc[...] = jnp.full(m_sc.shape, NEG_INF, F32)
        l_sc[...] = jnp.zeros(l_sc.shape, F32)
        acc_sc[...] = jnp.zeros(acc_sc.shape, F32)
        attend(kn_ref[...].astype(BF16), vn_ref[...].astype(BF16), bn_ref)

    attend(kc_ref[...].astype(BF16), vc_ref[...].astype(BF16), bc_ref)

    @pl.when(j == pl.num_programs(1) - 1)
    def _():
        lam = _diff_lambda(dl_ref, lam_init)
        for h in range(n_heads):
            o = acc_sc[2 * h] / l_sc[2 * h] - lam * (acc_sc[2 * h + 1] / l_sc[2 * h + 1])
            o_ref[:, h * hw:(h + 1) * hw] = (_rmsnorm(o, sg_ref[...]) * (1.0 - lam_init)).astype(o_ref.dtype)


def attn_decode(q, k_new, v_new, cache_k, cache_v, bias_new, bias_cache, dl, sg, n_heads, head_dim, lam_init):
    bsz, lq, width = q.shape
    past = cache_k.shape[1]
    tk = bias_cache.shape[-1]
    n_hm = 2 * n_heads
    hw = 2 * head_dim
    return pl.pallas_call(
        functools.partial(_attn_decode_kernel, n_heads=n_heads, head_dim=head_dim, lam_init=lam_init),
        grid=(bsz, past // tk),
        in_specs=[pl.BlockSpec((4, head_dim), lambda b, j: (0, 0)),
                  pl.BlockSpec((1, hw), lambda b, j: (0, 0)),
                  pl.BlockSpec((None, lq, width), lambda b, j: (b, 0, 0)),
                  pl.BlockSpec((None, lq, width), lambda b, j: (b, 0, 0)),
                  pl.BlockSpec((None, lq, width), lambda b, j: (b, 0, 0)),
                  pl.BlockSpec((None, tk, width), lambda b, j: (b, j, 0)),
                  pl.BlockSpec((None, tk, width), lambda b, j: (b, j, 0)),
                  pl.BlockSpec((n_hm, None, lq, lq), lambda b, j: (0, 0, 0, 0)),
                  pl.BlockSpec((n_hm, None, lq, tk), lambda b, j: (0, j, 0, 0))],
        out_specs=pl.BlockSpec((None, lq, width), lambda b, j: (b, 0, 0)),
        out_shape=jax.ShapeDtypeStruct((bsz, lq, width), BF16),
        scratch_shapes=[pltpu.VMEM((n_hm, lq, 1), F32), pltpu.VMEM((n_hm, lq, 1), F32),
                        pltpu.VMEM((n_hm, lq, hw), F32)],
        compiler_params=_cparams(("parallel", "arbitrary"), 48),
        name="attn_decode",
    )(dl, sg.reshape(1, hw), q, k_new, v_new, cache_k, cache_v, bias_new, bias_cache)


def _ssm_params(lam_re, lam_im, log_dt, b_re, b_im, c_re, c_im, d):
    g, p, hg = b_re.shape
    dt = jnp.exp(log_dt)[:, None]
    mag = jnp.exp(lam_re * dt)
    ang = lam_im * dt
    a_re = mag * jnp.cos(ang)
    a_im = mag * jnp.sin(ang)
    den = lam_re * lam_re + lam_im * lam_im
    n_re = a_re - 1.0
    n_im = a_im
    f_re = (n_re * lam_re + n_im * lam_im) / den
    f_im = (n_im * lam_re - n_re * lam_im) / den
    bb_re = f_re[..., None] * b_re - f_im[..., None] * b_im
    bb_im = f_re[..., None] * b_im + f_im[..., None] * b_re
    gb = min(SSM_GROUPS_PER_BLOCK, g)
    nb = g // gb
    eye = jnp.eye(gb, dtype=F32)

    def in_proj(bb):
        t = bb.reshape(nb, gb, p, hg).transpose(0, 1, 3, 2)
        return jnp.einsum("nghp,gk->nghkp", t, eye).reshape(nb, gb * hg, gb * p).astype(BF16)

    def out_proj(c):
        t = c.reshape(nb, gb, hg, p).transpose(0, 1, 3, 2)
        return jnp.einsum("ngph,gk->ngpkh", t, eye).reshape(nb, gb * p, gb * hg).astype(BF16)

    return dict(a_re=a_re.reshape(1, g * p), a_im=a_im.reshape(1, g * p),
                bbr=in_proj(bb_re), bbi=in_proj(bb_im),
                ccr=out_proj(c_re), cci=out_proj(-c_im), d=d.reshape(1, g * hg))


def kernel(x_prompt, x_sample, state_ssm_re, state_ssm_im, cache_k, cache_v, mix_norm_g, ffn_norm_g, ssm_lambda_re, ssm_lambda_im, ssm_log_dt, ssm_b_re, ssm_b_im, ssm_c_re, ssm_c_im, ssm_d, w_glu, kv_norm_g, w_kv, k_norm_g, w_q, q_norm_g, diff_lambda, subln_g, w_o, rel_bias, w_ffn_in, w_ffn_out):
    bp, seq, d = x_prompt.shape
    bs, dseq, _ = x_sample.shape
    depth = mix_norm_g.shape[0]
    n_a = ssm_lambda_re.shape[0]
    n_groups, n_state = ssm_lambda_re.shape[1:]
    past, n_heads, _, head_dim = cache_k.shape[1:]
    qk_width = n_heads * 2 * head_dim
    streams = SSM_STREAMS
    n_seg = streams // bp
    seg_len = seq // n_seg
    assert bs == streams and bp * n_seg == streams and seg_len % SSM_TIME_CHUNK == 0
    assert dseq % SSM_TIME_CHUNK == 0 and seq % ATTN_TILE == 0 and past % DEC_KEY_TILE == 0
    assert ATTN_TILE % CHUNK == 0 and ATTN_TILE >= MAX_DISTANCE

    hp = x_prompt.reshape(bp, n_seg, seg_len, d).transpose(2, 0, 1, 3).reshape(seg_len * streams, d)
    hs = x_sample.transpose(1, 0, 2).reshape(dseq * bs, d)
    zeros = jnp.zeros((streams, n_groups * n_state), F32)

    ssm_re_p, ssm_im_p, ssm_re_s, ssm_im_s = [], [], [], []
    k_p = v_p = k_s = v_s = None
    bias_p = bias_sn = bias_sc = None
    for layer in range(depth):
        gm = mix_norm_g[layer]
        if layer < n_a:
            prm = _ssm_params(ssm_lambda_re[layer], ssm_lambda_im[layer], ssm_log_dt[layer],
                              ssm_b_re[layer], ssm_b_im[layer], ssm_c_re[layer], ssm_c_im[layer],
                              ssm_d[layer])
            wg = w_glu[layer].astype(BF16)
            up = rmsnorm_rows(hp, gm)
            _, er, ei = ssm_scan(up, zeros, zeros, prm, emit_y=False)
            sr, si = seg_combine(er, ei, prm["a_re"], prm["a_im"], bp, n_seg, seg_len)
            gp, er, ei = ssm_scan(up, sr, si, prm, emit_y=True)
            last = er.reshape(bp, n_seg, n_groups, n_state)[:, -1], ei.reshape(bp, n_seg, n_groups, n_state)[:, -1]
            ssm_re_p.append(last[0])
            ssm_im_p.append(last[1])
            hp = glu_residual(gp, wg, hp)
            us = rmsnorm_rows(hs, gm)
            gs, er, ei = ssm_scan(us, state_ssm_re[layer].reshape(bs, -1), state_ssm_im[layer].reshape(bs, -1),
                                  prm, emit_y=True)
            ssm_re_s.append(er.reshape(bs, n_groups, n_state))
            ssm_im_s.append(ei.reshape(bs, n_groups, n_state))
            hs = glu_residual(gs, wg, hs)
        else:
            jb = layer - n_a
            lam_init = 0.8 - 0.6 * math.exp(-0.3 * layer)
            if layer == n_a:
                hp = hp.reshape(seg_len, bp, n_seg, d).transpose(1, 2, 0, 3).reshape(bp * seq, d)
                hs = hs.reshape(dseq, bs, d).transpose(1, 0, 2).reshape(bs * dseq, d)
                wk = w_kv[:, :qk_width].astype(BF16)
                wv = w_kv[:, qk_width:].astype(BF16)
                k_p = norm_proj_headnorm(hp, kv_norm_g, wk, k_norm_g, qk_width, 1.0, F32)
                v_p = norm_proj_headnorm(hp, kv_norm_g, wv, k_norm_g, 0, 1.0, F32)
                k_s = norm_proj_headnorm(hs, kv_norm_g, wk, k_norm_g, qk_width, 1.0, F32)
                v_s = norm_proj_headnorm(hs, kv_norm_g, wv, k_norm_g, 0, 1.0, F32)
                t = ATTN_TILE
                bias_p = bias_tiles(rel_bias, t, t, 3, 2 * t, 2 * t, -t)
                bias_sn = bias_tiles(rel_bias, dseq, dseq, 1, past, past, 0)
                bias_sc = bias_tiles(rel_bias, dseq, DEC_KEY_TILE, past // DEC_KEY_TILE, past, 0, DEC_KEY_TILE)
            wq = w_q[jb].astype(BF16)
            wo = w_o[jb].astype(BF16)
            scale = head_dim ** -0.5
            qp = norm_proj_headnorm(hp, gm, wq, q_norm_g[jb], qk_width, scale, BF16)
            op = attn_prompt(qp.reshape(bp, seq, qk_width), k_p.reshape(bp, seq, qk_width),
                             v_p.reshape(bp, seq, -1), bias_p, diff_lambda[jb], subln_g[jb],
                             n_heads, head_dim, lam_init)
            hp = matmul_residual(op.reshape(bp * seq, -1), wo, hp)
            qs = norm_proj_headnorm(hs, gm, wq, q_norm_g[jb], qk_width, scale, BF16)
            os_ = attn_decode(qs.reshape(bs, dseq, qk_width), k_s.reshape(bs, dseq, qk_width),
                              v_s.reshape(bs, dseq, -1), cache_k.reshape(bs, past, qk_width),
                              cache_v.reshape(bs, past, -1), bias_sn, bias_sc, diff_lambda[jb], subln_g[jb],
                              n_heads, head_dim, lam_init)
            hs = matmul_residual(os_.reshape(bs * dseq, -1), wo, hs)
        w_in = w_ffn_in[layer].astype(BF16)
        w_out = w_ffn_out[layer].astype(BF16)
        hp = ffn_residual(hp, ffn_norm_g[layer], w_in, w_out)
        hs = ffn_residual(hs, ffn_norm_g[layer], w_in, w_out)

    return (hp.reshape(bp, seq, d), hs.reshape(bs, dseq, d),
            jnp.stack(ssm_re_p), jnp.stack(ssm_im_p),
            k_p.reshape(bp, seq, n_heads, 2, head_dim), v_p.reshape(bp, seq, n_heads, 2 * head_dim),
            jnp.stack(ssm_re_s), jnp.stack(ssm_im_s),
            k_s.reshape(bs, dseq, n_heads, 2, head_dim), v_s.reshape(bs, dseq, n_heads, 2 * head_dim))
```

```python
import functools
import math

import jax
import jax.numpy as jnp
from jax import lax
from jax.experimental import pallas as pl
from jax.experimental.pallas import tpu as pltpu

F32 = jnp.float32
BF16 = jnp.bfloat16

EPS = 1e-6
NEG_INF = -1e30
CHUNK = 64
NUM_BUCKETS = 32
MAX_DISTANCE = 128

V7X_VMEM_BYTES = 64 * 1024 * 1024
LANES = 128
SUBLANES = 8

SSM_GROUPS_PER_BLOCK = 16
SSM_TIME_CHUNK = 16
SSM_STREAMS = 32
ROW_TILE = 512
COL_TILE = 512
ATTN_TILE = 256
DEC_KEY_TILE = 512


def _cparams(semantics, vmem_mb):
    return pltpu.CompilerParams(dimension_semantics=semantics,
                                vmem_limit_bytes=min(vmem_mb * 1024 * 1024, V7X_VMEM_BYTES))


def _rmsnorm(x, g):
    ms = jnp.mean(x * x, axis=-1, keepdims=True)
    return x * lax.rsqrt(ms + EPS) * g


def _dot(a, b):
    return jnp.dot(a, b, preferred_element_type=F32)


def _dot_nt(a, b):
    return lax.dot_general(a, b, (((1,), (1,)), ((), ())), preferred_element_type=F32)


def _norm_kernel(x_ref, g_ref, o_ref):
    o_ref[...] = _rmsnorm(x_ref[...], g_ref[...]).astype(o_ref.dtype)


def rmsnorm_rows(x, g, out_dtype=F32):
    n, d = x.shape
    tm = min(ROW_TILE, n)
    return pl.pallas_call(
        _norm_kernel,
        grid=(n // tm,),
        in_specs=[pl.BlockSpec((tm, d), lambda i: (i, 0)),
                  pl.BlockSpec((1, d), lambda i: (0, 0))],
        out_specs=pl.BlockSpec((tm, d), lambda i: (i, 0)),
        out_shape=jax.ShapeDtypeStruct((n, d), out_dtype),
        compiler_params=_cparams(("parallel",), 32),
        name="rmsnorm_rows",
    )(x, g.reshape(1, d))


def _ssm_kernel(*refs, streams, steps, emit_y):
    if emit_y:
        (u_ref, h0r_ref, h0i_ref, ar_ref, ai_ref, bbr_ref, bbi_ref, ccr_ref, cci_ref, d_ref,
         g_ref, her_ref, hei_ref, bur, bui) = refs
    else:
        (u_ref, h0r_ref, h0i_ref, ar_ref, ai_ref, bbr_ref, bbi_ref,
         her_ref, hei_ref, bur, bui) = refs

    @pl.when(pl.program_id(1) == 0)
    def _():
        her_ref[...] = h0r_ref[...]
        hei_ref[...] = h0i_ref[...]

    u = u_ref[...]
    ub = u.astype(BF16)
    bur[...] = _dot(ub, bbr_ref[...])
    bui[...] = _dot(ub, bbi_ref[...])

    n_state = bur.shape[1]
    for c in range(n_state // LANES):
        cols = slice(c * LANES, (c + 1) * LANES)
        ar = jnp.broadcast_to(ar_ref[:, cols], (streams, LANES))
        ai = jnp.broadcast_to(ai_ref[:, cols], (streams, LANES))
        hr = her_ref[:, cols]
        hi = hei_ref[:, cols]
        for t in range(steps):
            rows = slice(t * streams, (t + 1) * streams)
            nr = ar * hr - ai * hi + bur[rows, cols]
            ni = ar * hi + ai * hr + bui[rows, cols]
            bur[rows, cols] = nr
            bui[rows, cols] = ni
            hr, hi = nr, ni
        her_ref[:, cols] = hr
        hei_ref[:, cols] = hi

    if emit_y:
        y = (_dot(bur[...].astype(BF16), ccr_ref[...]) + _dot(bui[...].astype(BF16), cci_ref[...])
             + d_ref[...] * u)
        g_ref[...] = jax.nn.gelu(y).astype(g_ref.dtype)


def ssm_scan(u, h0r, h0i, prm, emit_y):
    n, d = u.shape
    s = h0r.shape[0]
    steps = SSM_TIME_CHUNK
    rows = steps * s
    nb, cb, sb = prm["bbr"].shape
    grid = (nb, n // rows)
    in_specs = [
        pl.BlockSpec((rows, cb), lambda b, t: (t, b)),
        pl.BlockSpec((s, sb), lambda b, t: (0, b)),
        pl.BlockSpec((s, sb), lambda b, t: (0, b)),
        pl.BlockSpec((1, sb), lambda b, t: (0, b)),
        pl.BlockSpec((1, sb), lambda b, t: (0, b)),
        pl.BlockSpec((None, cb, sb), lambda b, t: (b, 0, 0)),
        pl.BlockSpec((None, cb, sb), lambda b, t: (b, 0, 0)),
    ]
    args = [u, h0r, h0i, prm["a_re"], prm["a_im"], prm["bbr"], prm["bbi"]]
    state_spec = pl.BlockSpec((s, sb), lambda b, t: (0, b))
    state_shape = jax.ShapeDtypeStruct(h0r.shape, F32)
    if emit_y:
        in_specs += [
            pl.BlockSpec((None, sb, cb), lambda b, t: (b, 0, 0)),
            pl.BlockSpec((None, sb, cb), lambda b, t: (b, 0, 0)),
            pl.BlockSpec((1, cb), lambda b, t: (0, b)),
        ]
        args += [prm["ccr"], prm["cci"], prm["d"]]
        out_specs = [pl.BlockSpec((rows, cb), lambda b, t: (t, b)), state_spec, state_spec]
        out_shape = [jax.ShapeDtypeStruct((n, d), BF16), state_shape, state_shape]
    else:
        out_specs = [state_spec, state_spec]
        out_shape = [state_shape, state_shape]
    outs = pl.pallas_call(
        functools.partial(_ssm_kernel, streams=s, steps=steps, emit_y=emit_y),
        grid=grid,
        in_specs=in_specs,
        out_specs=out_specs,
        out_shape=out_shape,
        scratch_shapes=[pltpu.VMEM((rows, sb), F32), pltpu.VMEM((rows, sb), F32)],
        compiler_params=_cparams(("parallel", "arbitrary"), 40),
        name="ssm_scan_y" if emit_y else "ssm_scan_state",
    )(*args)
    if emit_y:
        return outs[0], outs[1], outs[2]
    return None, outs[0], outs[1]


def _seg_combine_kernel(er_ref, ei_ref, ar_ref, ai_ref, hr_ref, hi_ref, *, n_batch, n_seg, seg_len):
    pr, pi = ar_ref[...], ai_ref[...]
    tr, ti = None, None
    e = seg_len
    while e:
        if e & 1:
            if tr is None:
                tr, ti = pr, pi
            else:
                tr, ti = tr * pr - ti * pi, tr * pi + ti * pr
        e >>= 1
        if e:
            pr, pi = pr * pr - pi * pi, 2.0 * pr * pi
    for b in range(n_batch):
        hr = jnp.zeros_like(tr)
        hi = jnp.zeros_like(ti)
        for s in range(n_seg):
            row = slice(b * n_seg + s, b * n_seg + s + 1)
            hr_ref[row, :] = hr
            hi_ref[row, :] = hi
            hr, hi = tr * hr - ti * hi + er_ref[row, :], tr * hi + ti * hr + ei_ref[row, :]


def seg_combine(er, ei, a_re, a_im, n_batch, n_seg, seg_len):
    s, n = er.shape
    tn = min(1024, n)
    spec = pl.BlockSpec((s, tn), lambda j: (0, j))
    aspec = pl.BlockSpec((1, tn), lambda j: (0, j))
    return pl.pallas_call(
        functools.partial(_seg_combine_kernel, n_batch=n_batch, n_seg=n_seg, seg_len=seg_len),
        grid=(n // tn,),
        in_specs=[spec, spec, aspec, aspec],
        out_specs=[spec, spec],
        out_shape=[jax.ShapeDtypeStruct((s, n), F32)] * 2,
        compiler_params=_cparams(("parallel",), 32),
        name="ssm_seg_combine",
    )(er, ei, a_re, a_im)


def _glu_kernel(g_ref, wa_ref, wb_ref, x_ref, o_ref):
    g = g_ref[...]
    o_ref[...] = x_ref[...] + _dot(g, wa_ref[...]) * jax.nn.sigmoid(_dot(g, wb_ref[...]))


def glu_residual(g, w, x):
    n, k = g.shape
    d = x.shape[1]
    tm, tn = min(ROW_TILE, n), min(COL_TILE, d)
    nj = d // tn
    return pl.pallas_call(
        _glu_kernel,
        grid=(n // tm, nj),
        in_specs=[pl.BlockSpec((tm, k), lambda i, j: (i, 0)),
                  pl.BlockSpec((k, tn), lambda i, j: (0, j)),
                  pl.BlockSpec((k, tn), lambda i, j: (0, j + nj)),
                  pl.BlockSpec((tm, tn), lambda i, j: (i, j))],
        out_specs=pl.BlockSpec((tm, tn), lambda i, j: (i, j)),
        out_shape=jax.ShapeDtypeStruct((n, d), F32),
        compiler_params=_cparams(("parallel", "arbitrary"), 40),
        name="glu_residual",
    )(g, w, w, x)


def _mm_res_kernel(a_ref, w_ref, x_ref, o_ref):
    o_ref[...] = x_ref[...] + _dot(a_ref[...], w_ref[...])


def matmul_residual(a, w, x):
    n, k = a.shape
    d = w.shape[1]
    tm, tn = min(ROW_TILE, n), min(COL_TILE, d)
    return pl.pallas_call(
        _mm_res_kernel,
        grid=(n // tm, d // tn),
        in_specs=[pl.BlockSpec((tm, k), lambda i, j: (i, 0)),
                  pl.BlockSpec((k, tn), lambda i, j: (0, j)),
                  pl.BlockSpec((tm, tn), lambda i, j: (i, j))],
        out_specs=pl.BlockSpec((tm, tn), lambda i, j: (i, j)),
        out_shape=jax.ShapeDtypeStruct((n, d), F32),
        compiler_params=_cparams(("parallel", "arbitrary"), 40),
        name="matmul_residual",
    )(a, w, x)


def _ffn_kernel(x_ref, gn_ref, wg_ref, wu_ref, wo_ref, o_ref, xn_ref):
    @pl.when(pl.program_id(1) == 0)
    def _():
        x = x_ref[...]
        xn_ref[...] = _rmsnorm(x, gn_ref[...]).astype(BF16)
        o_ref[...] = x

    xn = xn_ref[...]
    hid = (jax.nn.silu(_dot(xn, wg_ref[...])) * _dot(xn, wu_ref[...])).astype(BF16)
    o_ref[...] += _dot(hid, wo_ref[...])


def ffn_residual(x, gn, w_in, w_out):
    n, d = x.shape
    f = w_out.shape[0]
    tm = min(ROW_TILE, n)
    tf = COL_TILE if f % COL_TILE == 0 else 256
    nf = f // tf
    return pl.pallas_call(
        _ffn_kernel,
        grid=(n // tm, nf),
        in_specs=[pl.BlockSpec((tm, d), lambda i, j: (i, 0)),
                  pl.BlockSpec((1, d), lambda i, j: (0, 0)),
                  pl.BlockSpec((d, tf), lambda i, j: (0, j)),
                  pl.BlockSpec((d, tf), lambda i, j: (0, j + nf)),
                  pl.BlockSpec((tf, d), lambda i, j: (j, 0))],
        out_specs=pl.BlockSpec((tm, d), lambda i, j: (i, 0)),
        out_shape=jax.ShapeDtypeStruct((n, d), F32),
        scratch_shapes=[pltpu.VMEM((tm, d), BF16)],
        compiler_params=_cparams(("parallel", "arbitrary"), 48),
        name="ffn_residual",
    )(x, gn.reshape(1, d), w_in, w_in, w_out)


def _proj_kernel(x_ref, gn_ref, w_ref, hg_ref, *refs, n_norm_blocks, head_dim, scale):
    out_refs, xn_ref = refs[:-1], refs[-1]
    j = pl.program_id(1)

    @pl.when(j == 0)
    def _():
        xn_ref[...] = _rmsnorm(x_ref[...], gn_ref[...]).astype(BF16)

    acc = _dot(xn_ref[...], w_ref[...])
    tn = acc.shape[1]

    @pl.when(j < n_norm_blocks)
    def _():
        for c in range(tn // head_dim):
            cols = slice(c * head_dim, (c + 1) * head_dim)
            y = _rmsnorm(acc[:, cols], hg_ref[...])
            if scale != 1.0:
                y = y * scale
            for o_ref in out_refs:
                o_ref[:, cols] = y.astype(o_ref.dtype)

    @pl.when(j >= n_norm_blocks)
    def _():
        for o_ref in out_refs:
            o_ref[...] = acc.astype(o_ref.dtype)


def norm_proj_headnorm(x, gn, w, hg, n_norm_cols, scale, out_dtypes):
    n, d = x.shape
    m = w.shape[1]
    hd = hg.shape[0]
    tm, tn = min(ROW_TILE, n), min(COL_TILE, m)
    assert n_norm_cols % tn == 0 and tn % hd == 0
    out_spec = pl.BlockSpec((tm, tn), lambda i, j: (i, j))
    return pl.pallas_call(
        functools.partial(_proj_kernel, n_norm_blocks=n_norm_cols // tn, head_dim=hd, scale=scale),
        grid=(n // tm, m // tn),
        in_specs=[pl.BlockSpec((tm, d), lambda i, j: (i, 0)),
                  pl.BlockSpec((1, d), lambda i, j: (0, 0)),
                  pl.BlockSpec((d, tn), lambda i, j: (0, j)),
                  pl.BlockSpec((1, hd), lambda i, j: (0, 0))],
        out_specs=[out_spec] * len(out_dtypes),
        out_shape=[jax.ShapeDtypeStruct((n, m), dt) for dt in out_dtypes],
        scratch_shapes=[pltpu.VMEM((tm, d), BF16)],
        compiler_params=_cparams(("parallel", "arbitrary"), 40),
        name="norm_proj_headnorm",
    )(x, gn.reshape(1, d), w, hg.reshape(1, hd))


def _bucket_thresholds():
    half = NUM_BUCKETS // 2
    max_exact = half // 2
    steps = half - max_exact
    thr = []
    for j in range(1, steps):
        n = max_exact
        while (n ** steps) * (max_exact ** j) < (max_exact ** steps) * (MAX_DISTANCE ** j):
            n += 1
        thr.append(n)
    return max_exact, thr


def _bias_kernel(rb_ref, o_ref, *, n_hm, q0, k_start, k_stride):
    hm = pl.program_id(0)
    tq, tk = o_ref.shape
    k0 = k_start + pl.program_id(1) * k_stride
    qpos = q0 + lax.broadcasted_iota(jnp.int32, (tq, tk), 0)
    kpos = k0 + lax.broadcasted_iota(jnp.int32, (tq, tk), 1)
    n = qpos - kpos
    na = jnp.abs(n)
    max_exact, thr = _bucket_thresholds()
    large = jnp.full((tq, tk), max_exact, jnp.int32)
    for t in thr:
        large = large + jnp.where(na >= t, 1, 0)
    bucket = jnp.where(n < 0, NUM_BUCKETS // 2, 0) + jnp.where(na < max_exact, na, large)
    val = jnp.zeros((tq, tk), F32)
    for b in range(NUM_BUCKETS):
        val = jnp.where(bucket == b, rb_ref[b * n_hm + hm], val)
    shift = CHUNK.bit_length() - 1
    visible = lax.shift_right_logical(kpos, shift) <= lax.shift_right_logical(qpos, shift)
    o_ref[...] = jnp.where(visible, val, NEG_INF)


def bias_tiles(rel_bias, tq, tk, n_tiles, q0, k_start, k_stride):
    n_heads, n_maps = rel_bias.shape[1:]
    n_hm = n_heads * n_maps
    return pl.pallas_call(
        functools.partial(_bias_kernel, n_hm=n_hm, q0=q0, k_start=k_start, k_stride=k_stride),
        grid=(n_hm, n_tiles),
        in_specs=[pl.BlockSpec(memory_space=pltpu.SMEM)],
        out_specs=pl.BlockSpec((None, None, tq, tk), lambda a, t: (a // n_maps, t, a % n_maps, 0)),
        out_shape=jax.ShapeDtypeStruct((n_heads, n_tiles, n_maps * tq, tk), F32),
        compiler_params=_cparams(("parallel", "parallel"), 32),
        name="rel_bias_tiles",
    )(rel_bias.reshape(-1))


def _diff_lambda(dl_ref, lam_init):
    dl = dl_ref[...]
    return (jnp.exp(jnp.sum(dl[0:1] * dl[1:2], axis=-1, keepdims=True))
            - jnp.exp(jnp.sum(dl[2:3] * dl[3:4], axis=-1, keepdims=True)) + lam_init)


def _softmax_update(s, v, m_ref, l_ref, acc_ref, shift=None):
    m_prev = m_ref[...]
    row_max = jnp.max(s, axis=-1, keepdims=True)
    if shift is not None:
        row_max = row_max + shift
    m_new = jnp.maximum(m_prev, row_max)
    alpha = jnp.exp(m_prev - m_new)
    p = jnp.exp(s - (m_new if shift is None else m_new - shift))
    l_ref[...] = alpha * l_ref[...] + jnp.sum(p, axis=-1, keepdims=True)
    acc_ref[...] = alpha * acc_ref[...] + _dot(p.astype(BF16), v)
    m_ref[...] = m_new


def _diff_combine(acc, l, lam, sg, lam_init, rows):
    o = acc[:rows] / l[:rows] - lam * (acc[rows:] / l[rows:])
    return _rmsnorm(o, sg) * (1.0 - lam_init)


def _lane_tile(x, width):
    return jnp.concatenate([x] * (width // LANES), axis=1)


def _softmax_update_wide(s, v, m_ref, l_ref, acc_ref, shift=None):
    w = s.shape[1]
    m_prev = m_ref[...]
    part = s[:, :LANES]
    for c in range(1, w // LANES):
        part = jnp.maximum(part, s[:, c * LANES:(c + 1) * LANES])
    row_max = jnp.max(part, axis=-1, keepdims=True)
    if shift is None:
        m_new = jnp.maximum(m_prev, row_max)
        base = m_new
    else:
        m_new = jnp.maximum(m_prev, row_max + shift)
        base = m_new - shift
    alpha = jnp.exp(m_prev - m_new)
    p = jnp.exp(s - _lane_tile(base, w))
    psum = p[:, :LANES]
    for c in range(1, w // LANES):
        psum = psum + p[:, c * LANES:(c + 1) * LANES]
    l_ref[...] = alpha * l_ref[...] + psum
    acc_ref[...] = _lane_tile(alpha, acc_ref.shape[1]) * acc_ref[...] + _dot(p.astype(BF16), v)
    m_ref[...] = m_new


PROMPT_FAR_SPANS = (4, 2, 1)


def _attn_prompt_kernel(dl_ref, sg_ref, q_ref, k_ref, v_ref, b_ref, o_ref, m_sc, l_sc, acc_sc,
                        *, head_dim, lam_init):
    i = pl.program_id(2)
    t = q_ref.shape[0]
    m_sc[...] = jnp.full(m_sc.shape, NEG_INF, F32)
    l_sc[...] = jnp.zeros(l_sc.shape, F32)
    acc_sc[...] = jnp.zeros(acc_sc.shape, F32)
    q0 = q_ref[:, :head_dim]
    q1 = q_ref[:, head_dim:]

    def logits(first_tile, n_tiles):
        rows = pl.ds(pl.multiple_of(first_tile * t, t), n_tiles * t)
        k = k_ref[rows, :]
        s = jnp.concatenate([_dot_nt(q0, k[:, :head_dim]), _dot_nt(q1, k[:, head_dim:])], axis=0)
        return s, v_ref[rows, :]

    far_shift = b_ref[2, :, 0:LANES]
    n_far = jnp.maximum(i - 1, 0)
    done = 0
    for span in PROMPT_FAR_SPANS:
        count = (n_far - done) // span

        def far_body(g, carry, span=span, done=done):
            s, v = logits(done + g * span, span)
            _softmax_update_wide(s, v, m_sc, l_sc, acc_sc, shift=far_shift)
            return carry

        lax.fori_loop(0, count, far_body, 0)
        done = done + count * span

    @pl.when(i >= 1)
    def _():
        s, v = logits(i - 1, 2)
        bias = jnp.concatenate([b_ref[1], b_ref[0]], axis=1)
        _softmax_update_wide(s + bias, v, m_sc, l_sc, acc_sc)

    @pl.when(i == 0)
    def _():
        s, v = logits(0, 1)
        _softmax_update_wide(s + b_ref[0], v, m_sc, l_sc, acc_sc)

    lam = _diff_lambda(dl_ref, lam_init)
    l = jnp.sum(l_sc[...], axis=-1, keepdims=True)
    o_ref[...] = _diff_combine(acc_sc[...], l, lam, sg_ref[...], lam_init, t).astype(o_ref.dtype)


def attn_prompt(q, k, v, bias, dl, sg, n_heads, head_dim, lam_init):
    bsz, seq, width = q.shape
    t = bias.shape[-1]
    hw = 2 * head_dim
    return pl.pallas_call(
        functools.partial(_attn_prompt_kernel, head_dim=head_dim, lam_init=lam_init),
        grid=(bsz, n_heads, seq // t),
        in_specs=[pl.BlockSpec((4, head_dim), lambda b, h, i: (0, 0)),
                  pl.BlockSpec((1, hw), lambda b, h, i: (0, 0)),
                  pl.BlockSpec((None, t, hw), lambda b, h, i: (b, i, h)),
                  pl.BlockSpec((None, seq, hw), lambda b, h, i: (b, 0, h)),
                  pl.BlockSpec((None, seq, hw), lambda b, h, i: (b, 0, h)),
                  pl.BlockSpec((None, 3, 2 * t, t), lambda b, h, i: (h, 0, 0, 0))],
        out_specs=pl.BlockSpec((None, t, hw), lambda b, h, i: (b, i, h)),
        out_shape=jax.ShapeDtypeStruct((bsz, seq, width), BF16),
        scratch_shapes=[pltpu.VMEM((2 * t, LANES), F32), pltpu.VMEM((2 * t, LANES), F32),
                        pltpu.VMEM((2 * t, hw), F32)],
        compiler_params=_cparams(("parallel", "parallel", "arbitrary"), 40),
        name="attn_prompt",
    )(dl, sg.reshape(1, hw), q, k, v, bias)


def _attn_decode_kernel(dl_ref, sg_ref, q_ref, kn_ref, vn_ref, bn_ref, bc_ref, kc_hbm, vc_hbm, o_ref,
                        kbuf, vbuf, sem, m_sc, l_sc, acc_sc, *, n_heads, head_dim, lam_init):
    b, j = pl.program_id(0), pl.program_id(1)
    nk = pl.num_programs(1)
    n_steps = pl.num_programs(0) * nk
    step = b * nk + j
    slot = lax.rem(step, 2)
    tk = kbuf.shape[2]
    lq = q_ref.shape[0]
    hw = 2 * head_dim

    def tile_copies(bb, jj, sl):
        rows = pl.ds(pl.multiple_of(jj * tk, tk), tk)
        cps = [pltpu.make_async_copy(kc_hbm.at[bb, rows, hm, :], kbuf.at[sl, hm], sem.at[sl])
               for hm in range(2 * n_heads)]
        cps += [pltpu.make_async_copy(vc_hbm.at[bb, rows, h, :], vbuf.at[sl, h], sem.at[sl])
                for h in range(n_heads)]
        return cps

    @pl.when(step == 0)
    def _():
        for cp in tile_copies(b, j, slot):
            cp.start()

    @pl.when(step + 1 < n_steps)
    def _():
        wrap = j + 1 == nk
        for cp in tile_copies(jnp.where(wrap, b + 1, b), jnp.where(wrap, 0, j + 1), 1 - slot):
            cp.start()

    def attend(h, k0, k1, v, bias):
        q0 = q_ref[:, (2 * h) * head_dim:(2 * h + 1) * head_dim]
        q1 = q_ref[:, (2 * h + 1) * head_dim:(2 * h + 2) * head_dim]
        s = jnp.concatenate([_dot_nt(q0, k0), _dot_nt(q1, k1)], axis=0) + bias
        _softmax_update(s, v, m_sc.at[h], l_sc.at[h], acc_sc.at[h])

    @pl.when(j == 0)
    def _():
        m_sc[...] = jnp.full(m_sc.shape, NEG_INF, F32)
        l_sc[...] = jnp.zeros(l_sc.shape, F32)
        acc_sc[...] = jnp.zeros(acc_sc.shape, F32)
        kn = kn_ref[...].astype(BF16)
        vn = vn_ref[...].astype(BF16)
        for h in range(n_heads):
            attend(h, kn[:, (2 * h) * head_dim:(2 * h + 1) * head_dim],
                   kn[:, (2 * h + 1) * head_dim:(2 * h + 2) * head_dim], vn[:, h * hw:(h + 1) * hw], bn_ref[h])

    for cp in tile_copies(b, j, slot):
        cp.wait()
    for h in range(n_heads):
        attend(h, kbuf[slot, 2 * h].astype(BF16), kbuf[slot, 2 * h + 1].astype(BF16),
               vbuf[slot, h].astype(BF16), bc_ref[h])

    @pl.when(j == nk - 1)
    def _():
        lam = _diff_lambda(dl_ref, lam_init)
        for h in range(n_heads):
            o_ref[:, h * hw:(h + 1) * hw] = _diff_combine(
                acc_sc[h], l_sc[h], lam, sg_ref[...], lam_init, lq).astype(o_ref.dtype)


def attn_decode(q, k_new, v_new, cache_k, cache_v, bias_new, bias_cache, dl, sg, n_heads, head_dim, lam_init):
    bsz, lq, width = q.shape
    past = cache_k.shape[1]
    tk = bias_cache.shape[-1]
    hw = 2 * head_dim
    return pl.pallas_call(
        functools.partial(_attn_decode_kernel, n_heads=n_heads, head_dim=head_dim, lam_init=lam_init),
        grid=(bsz, past // tk),
        in_specs=[pl.BlockSpec((4, head_dim), lambda b, j: (0, 0)),
                  pl.BlockSpec((1, hw), lambda b, j: (0, 0)),
                  pl.BlockSpec((None, lq, width), lambda b, j: (b, 0, 0)),
                  pl.BlockSpec((None, lq, width), lambda b, j: (b, 0, 0)),
                  pl.BlockSpec((None, lq, width), lambda b, j: (b, 0, 0)),
                  pl.BlockSpec((n_heads, None, 2 * lq, lq), lambda b, j: (0, 0, 0, 0)),
                  pl.BlockSpec((n_heads, None, 2 * lq, tk), lambda b, j: (0, j, 0, 0)),
                  pl.BlockSpec(memory_space=pl.ANY),
                  pl.BlockSpec(memory_space=pl.ANY)],
        out_specs=pl.BlockSpec((None, lq, width), lambda b, j: (b, 0, 0)),
        out_shape=jax.ShapeDtypeStruct((bsz, lq, width), BF16),
        scratch_shapes=[pltpu.VMEM((2, 2 * n_heads, tk, head_dim), F32),
                        pltpu.VMEM((2, n_heads, tk, hw), F32),
                        pltpu.SemaphoreType.DMA((2,)),
                        pltpu.VMEM((n_heads, 2 * lq, 1), F32), pltpu.VMEM((n_heads, 2 * lq, 1), F32),
                        pltpu.VMEM((n_heads, 2 * lq, hw), F32)],
        compiler_params=_cparams(("arbitrary", "arbitrary"), 48),
        name="attn_decode",
    )(dl, sg.reshape(1, hw), q, k_new, v_new, bias_new, bias_cache, cache_k, cache_v)


def _ssm_params(lam_re, lam_im, log_dt, b_re, b_im, c_re, c_im, d):
    g, p, hg = b_re.shape
    dt = jnp.exp(log_dt)[:, None]
    mag = jnp.exp(lam_re * dt)
    ang = lam_im * dt
    a_re = mag * jnp.cos(ang)
    a_im = mag * jnp.sin(ang)
    den = lam_re * lam_re + lam_im * lam_im
    n_re = a_re - 1.0
    n_im = a_im
    f_re = (n_re * lam_re + n_im * lam_im) / den
    f_im = (n_im * lam_re - n_re * lam_im) / den
    bb_re = f_re[..., None] * b_re - f_im[..., None] * b_im
    bb_im = f_re[..., None] * b_im + f_im[..., None] * b_re
    gb = min(SSM_GROUPS_PER_BLOCK, g)
    nb = g // gb
    eye = jnp.eye(gb, dtype=F32)

    def in_proj(bb):
        t = bb.reshape(nb, gb, p, hg).transpose(0, 1, 3, 2)
        return jnp.einsum("nghp,gk->nghkp", t, eye).reshape(nb, gb * hg, gb * p).astype(BF16)

    def out_proj(c):
        t = c.reshape(nb, gb, hg, p).transpose(0, 1, 3, 2)
        return jnp.einsum("ngph,gk->ngpkh", t, eye).reshape(nb, gb * p, gb * hg).astype(BF16)

    return dict(a_re=a_re.reshape(1, g * p), a_im=a_im.reshape(1, g * p),
                bbr=in_proj(bb_re), bbi=in_proj(bb_im),
                ccr=out_proj(c_re), cci=out_proj(-c_im), d=d.reshape(1, g * hg))


def kernel(x_prompt, x_sample, state_ssm_re, state_ssm_im, cache_k, cache_v, mix_norm_g, ffn_norm_g, ssm_lambda_re, ssm_lambda_im, ssm_log_dt, ssm_b_re, ssm_b_im, ssm_c_re, ssm_c_im, ssm_d, w_glu, kv_norm_g, w_kv, k_norm_g, w_q, q_norm_g, diff_lambda, subln_g, w_o, rel_bias, w_ffn_in, w_ffn_out):
    bp, seq, d = x_prompt.shape
    bs, dseq, _ = x_sample.shape
    depth = mix_norm_g.shape[0]
    n_a = ssm_lambda_re.shape[0]
    n_groups, n_state = ssm_lambda_re.shape[1:]
    past, n_heads, _, head_dim = cache_k.shape[1:]
    qk_width = n_heads * 2 * head_dim
    streams = SSM_STREAMS
    n_seg = streams // bp
    seg_len = seq // n_seg
    assert bs == streams and bp * n_seg == streams and seg_len % SSM_TIME_CHUNK == 0
    assert dseq % SSM_TIME_CHUNK == 0 and seq % ATTN_TILE == 0 and past % DEC_KEY_TILE == 0
    assert ATTN_TILE % CHUNK == 0 and ATTN_TILE >= MAX_DISTANCE

    hp = x_prompt.reshape(bp, n_seg, seg_len, d).transpose(2, 0, 1, 3).reshape(seg_len * streams, d)
    hs = x_sample.transpose(1, 0, 2).reshape(dseq * bs, d)
    zeros = jnp.zeros((streams, n_groups * n_state), F32)

    ssm_re_p, ssm_im_p, ssm_re_s, ssm_im_s = [], [], [], []
    k_p = v_p = k_s = v_s = None
    bias_p = bias_sn = bias_sc = None
    for layer in range(depth):
        gm = mix_norm_g[layer]
        if layer < n_a:
            prm = _ssm_params(ssm_lambda_re[layer], ssm_lambda_im[layer], ssm_log_dt[layer],
                              ssm_b_re[layer], ssm_b_im[layer], ssm_c_re[layer], ssm_c_im[layer],
                              ssm_d[layer])
            wg = w_glu[layer].astype(BF16)
            up = rmsnorm_rows(hp, gm)
            _, er, ei = ssm_scan(up, zeros, zeros, prm, emit_y=False)
            sr, si = seg_combine(er, ei, prm["a_re"], prm["a_im"], bp, n_seg, seg_len)
            gp, er, ei = ssm_scan(up, sr, si, prm, emit_y=True)
            last = er.reshape(bp, n_seg, n_groups, n_state)[:, -1], ei.reshape(bp, n_seg, n_groups, n_state)[:, -1]
            ssm_re_p.append(last[0])
            ssm_im_p.append(last[1])
            hp = glu_residual(gp, wg, hp)
            us = rmsnorm_rows(hs, gm)
            gs, er, ei = ssm_scan(us, state_ssm_re[layer].reshape(bs, -1), state_ssm_im[layer].reshape(bs, -1),
                                  prm, emit_y=True)
            ssm_re_s.append(er.reshape(bs, n_groups, n_state))
            ssm_im_s.append(ei.reshape(bs, n_groups, n_state))
            hs = glu_residual(gs, wg, hs)
        else:
            jb = layer - n_a
            lam_init = 0.8 - 0.6 * math.exp(-0.3 * layer)
            if layer == n_a:
                hp = hp.reshape(seg_len, bp, n_seg, d).transpose(1, 2, 0, 3).reshape(bp * seq, d)
                hs = hs.reshape(dseq, bs, d).transpose(1, 0, 2).reshape(bs * dseq, d)
                wk = w_kv[:, :qk_width].astype(BF16)
                wv = w_kv[:, qk_width:].astype(BF16)
                k_p, kb_p = norm_proj_headnorm(hp, kv_norm_g, wk, k_norm_g, qk_width, 1.0, (F32, BF16))
                v_p, vb_p = norm_proj_headnorm(hp, kv_norm_g, wv, k_norm_g, 0, 1.0, (F32, BF16))
                k_s, = norm_proj_headnorm(hs, kv_norm_g, wk, k_norm_g, qk_width, 1.0, (F32,))
                v_s, = norm_proj_headnorm(hs, kv_norm_g, wv, k_norm_g, 0, 1.0, (F32,))
                t = ATTN_TILE
                bias_p = bias_tiles(rel_bias, t, t, 3, 2 * t, 2 * t, -t)
                bias_sn = bias_tiles(rel_bias, dseq, dseq, 1, past, past, 0)
                bias_sc = bias_tiles(rel_bias, dseq, DEC_KEY_TILE, past // DEC_KEY_TILE, past, 0, DEC_KEY_TILE)
            wq = w_q[jb].astype(BF16)
            wo = w_o[jb].astype(BF16)
            scale = head_dim ** -0.5
            qp, = norm_proj_headnorm(hp, gm, wq, q_norm_g[jb], qk_width, scale, (BF16,))
            op = attn_prompt(qp.reshape(bp, seq, qk_width), kb_p.reshape(bp, seq, qk_width),
                             vb_p.reshape(bp, seq, -1), bias_p, diff_lambda[jb], subln_g[jb],
                             n_heads, head_dim, lam_init)
            hp = matmul_residual(op.reshape(bp * seq, -1), wo, hp)
            qs, = norm_proj_headnorm(hs, gm, wq, q_norm_g[jb], qk_width, scale, (BF16,))
            os_ = attn_decode(qs.reshape(bs, dseq, qk_width), k_s.reshape(bs, dseq, qk_width),
                              v_s.reshape(bs, dseq, -1), cache_k.reshape(bs, past, 2 * n_heads, head_dim),
                              cache_v, bias_sn, bias_sc, diff_lambda[jb], subln_g[jb],
                              n_heads, head_dim, lam_init)
            hs = matmul_residual(os_.reshape(bs * dseq, -1), wo, hs)
        w_in = w_ffn_in[layer].astype(BF16)
        w_out = w_ffn_out[layer].astype(BF16)
        hp = ffn_residual(hp, ffn_norm_g[layer], w_in, w_out)
        hs = ffn_residual(hs, ffn_norm_g[layer], w_in, w_out)

    return (hp.reshape(bp, seq, d), hs.reshape(bs, dseq, d),
            jnp.stack(ssm_re_p), jnp.stack(ssm_im_p),
            k_p.reshape(bp, seq, n_heads, 2, head_dim), v_p.reshape(bp, seq, n_heads, 2 * head_dim),
            jnp.stack(ssm_re_s), jnp.stack(ssm_im_s),
            k_s.reshape(bs, dseq, n_heads, 2, head_dim), v_s.reshape(bs, dseq, n_heads, 2 * head_dim))
```

```python
import functools
import math

import jax
import jax.numpy as jnp
from jax import lax
from jax.experimental import pallas as pl
from jax.experimental.pallas import tpu as pltpu

F32 = jnp.float32
BF16 = jnp.bfloat16

EPS = 1e-6
NEG_INF = -1e30
CHUNK = 64
NUM_BUCKETS = 32
MAX_DISTANCE = 128

V7X_VMEM_BYTES = 64 * 1024 * 1024
LANES = 128
SUBLANES = 8

SSM_GROUPS_PER_BLOCK = 16
SSM_TIME_CHUNK = 16
SSM_STREAMS = 32
SSM_SCAN_LANES = 256
ROW_TILE = 512
COL_TILE = 512
PROJ_COL_TILE = 1024
FFN_ROW_TILE = 1024
ATTN_TILE = 256
DEC_KEY_TILE = 512


def _cparams(semantics, vmem_mb):
    return pltpu.CompilerParams(dimension_semantics=semantics,
                                vmem_limit_bytes=min(vmem_mb * 1024 * 1024, V7X_VMEM_BYTES))


def _rmsnorm(x, g):
    ms = jnp.mean(x * x, axis=-1, keepdims=True)
    return x * lax.rsqrt(ms + EPS) * g


def _dot(a, b):
    return jnp.dot(a, b, preferred_element_type=F32)


def _dot_nt(a, b):
    return lax.dot_general(a, b, (((1,), (1,)), ((), ())), preferred_element_type=F32)


def _norm_kernel(x_ref, g_ref, o_ref):
    o_ref[...] = _rmsnorm(x_ref[...], g_ref[...]).astype(o_ref.dtype)


def rmsnorm_rows(x, g, out_dtype=F32):
    n, d = x.shape
    tm = min(ROW_TILE, n)
    return pl.pallas_call(
        _norm_kernel,
        grid=(n // tm,),
        in_specs=[pl.BlockSpec((tm, d), lambda i: (i, 0)),
                  pl.BlockSpec((1, d), lambda i: (0, 0))],
        out_specs=pl.BlockSpec((tm, d), lambda i: (i, 0)),
        out_shape=jax.ShapeDtypeStruct((n, d), out_dtype),
        compiler_params=_cparams(("parallel",), 32),
        name="rmsnorm_rows",
    )(x, g.reshape(1, d))


def _ssm_kernel(*refs, streams, steps, emit_y):
    if emit_y:
        (u_ref, h0r_ref, h0i_ref, ar_ref, ai_ref, bbr_ref, bbi_ref, ccr_ref, cci_ref, d_ref,
         g_ref, her_ref, hei_ref) = refs
    else:
        (u_ref, h0r_ref, h0i_ref, ar_ref, ai_ref, bbr_ref, bbi_ref, her_ref, hei_ref) = refs

    @pl.when(pl.program_id(1) == 0)
    def _():
        her_ref[...] = h0r_ref[...]
        hei_ref[...] = h0i_ref[...]

    u = u_ref[...]
    ub = u.astype(BF16)
    n_chan, n_state = bbr_ref.shape
    width = SSM_SCAN_LANES
    chan_per_chunk = width * n_chan // n_state
    assert LANES % chan_per_chunk == 0
    ys = [None] * (n_chan // LANES)
    for c in range(n_state // width):
        cols = slice(c * width, (c + 1) * width)
        slab = (c * chan_per_chunk) // LANES
        chans = slice(slab * LANES, (slab + 1) * LANES)
        bu_re = _dot(ub[:, chans], bbr_ref[chans, cols])
        bu_im = _dot(ub[:, chans], bbi_ref[chans, cols])
        ar = jnp.broadcast_to(ar_ref[:, cols], (streams, width))
        ai = jnp.broadcast_to(ai_ref[:, cols], (streams, width))
        hr = her_ref[:, cols]
        hi = hei_ref[:, cols]
        hrs, his = [], []
        for t in range(steps):
            rows = slice(t * streams, (t + 1) * streams)
            hr, hi = ar * hr - ai * hi + bu_re[rows], ar * hi + ai * hr + bu_im[rows]
            hrs.append(hr)
            his.append(hi)
        her_ref[:, cols] = hr
        hei_ref[:, cols] = hi
        if emit_y:
            part = (_dot(jnp.concatenate(hrs, axis=0).astype(BF16), ccr_ref[cols, chans])
                    + _dot(jnp.concatenate(his, axis=0).astype(BF16), cci_ref[cols, chans]))
            ys[slab] = part if ys[slab] is None else ys[slab] + part

    if emit_y:
        y = jnp.concatenate(ys, axis=1) + d_ref[...] * u
        g_ref[...] = jax.nn.gelu(y).astype(g_ref.dtype)


def ssm_scan(u, h0r, h0i, prm, emit_y):
    n, d = u.shape
    s = h0r.shape[0]
    steps = SSM_TIME_CHUNK
    rows = steps * s
    nb, cb, sb = prm["bbr"].shape
    grid = (nb, n // rows)
    in_specs = [
        pl.BlockSpec((rows, cb), lambda b, t: (t, b)),
        pl.BlockSpec((s, sb), lambda b, t: (0, b)),
        pl.BlockSpec((s, sb), lambda b, t: (0, b)),
        pl.BlockSpec((1, sb), lambda b, t: (0, b)),
        pl.BlockSpec((1, sb), lambda b, t: (0, b)),
        pl.BlockSpec((None, cb, sb), lambda b, t: (b, 0, 0)),
        pl.BlockSpec((None, cb, sb), lambda b, t: (b, 0, 0)),
    ]
    args = [u, h0r, h0i, prm["a_re"], prm["a_im"], prm["bbr"], prm["bbi"]]
    state_spec = pl.BlockSpec((s, sb), lambda b, t: (0, b))
    state_shape = jax.ShapeDtypeStruct(h0r.shape, F32)
    if emit_y:
        in_specs += [
            pl.BlockSpec((None, sb, cb), lambda b, t: (b, 0, 0)),
            pl.BlockSpec((None, sb, cb), lambda b, t: (b, 0, 0)),
            pl.BlockSpec((1, cb), lambda b, t: (0, b)),
        ]
        args += [prm["ccr"], prm["cci"], prm["d"]]
        out_specs = [pl.BlockSpec((rows, cb), lambda b, t: (t, b)), state_spec, state_spec]
        out_shape = [jax.ShapeDtypeStruct((n, d), BF16), state_shape, state_shape]
    else:
        out_specs = [state_spec, state_spec]
        out_shape = [state_shape, state_shape]
    outs = pl.pallas_call(
        functools.partial(_ssm_kernel, streams=s, steps=steps, emit_y=emit_y),
        grid=grid,
        in_specs=in_specs,
        out_specs=out_specs,
        out_shape=out_shape,
        compiler_params=_cparams(("parallel", "arbitrary"), 40),
        name="ssm_scan_y" if emit_y else "ssm_scan_state",
    )(*args)
    if emit_y:
        return outs[0], outs[1], outs[2]
    return None, outs[0], outs[1]


def _seg_combine_kernel(er_ref, ei_ref, ar_ref, ai_ref, hr_ref, hi_ref, *, n_batch, n_seg, seg_len):
    pr, pi = ar_ref[...], ai_ref[...]
    tr, ti = None, None
    e = seg_len
    while e:
        if e & 1:
            if tr is None:
                tr, ti = pr, pi
            else:
                tr, ti = tr * pr - ti * pi, tr * pi + ti * pr
        e >>= 1
        if e:
            pr, pi = pr * pr - pi * pi, 2.0 * pr * pi
    for b in range(n_batch):
        hr = jnp.zeros_like(tr)
        hi = jnp.zeros_like(ti)
        for s in range(n_seg):
            row = slice(b * n_seg + s, b * n_seg + s + 1)
            hr_ref[row, :] = hr
            hi_ref[row, :] = hi
            hr, hi = tr * hr - ti * hi + er_ref[row, :], tr * hi + ti * hr + ei_ref[row, :]


def seg_combine(er, ei, a_re, a_im, n_batch, n_seg, seg_len):
    s, n = er.shape
    tn = min(1024, n)
    spec = pl.BlockSpec((s, tn), lambda j: (0, j))
    aspec = pl.BlockSpec((1, tn), lambda j: (0, j))
    return pl.pallas_call(
        functools.partial(_seg_combine_kernel, n_batch=n_batch, n_seg=n_seg, seg_len=seg_len),
        grid=(n // tn,),
        in_specs=[spec, spec, aspec, aspec],
        out_specs=[spec, spec],
        out_shape=[jax.ShapeDtypeStruct((s, n), F32)] * 2,
        compiler_params=_cparams(("parallel",), 32),
        name="ssm_seg_combine",
    )(er, ei, a_re, a_im)


def _glu_kernel(g_ref, wa_ref, wb_ref, x_ref, o_ref):
    g = g_ref[...]
    o_ref[...] = x_ref[...] + _dot(g, wa_ref[...]) * jax.nn.sigmoid(_dot(g, wb_ref[...]))


def glu_residual(g, w, layer, x):
    n, k = g.shape
    d = x.shape[1]
    tm, tn = min(ROW_TILE, n), min(COL_TILE, d)
    nj = d // tn
    return pl.pallas_call(
        _glu_kernel,
        grid=(n // tm, nj),
        in_specs=[pl.BlockSpec((tm, k), lambda i, j: (i, 0)),
                  pl.BlockSpec((None, k, tn), lambda i, j: (layer, 0, j)),
                  pl.BlockSpec((None, k, tn), lambda i, j: (layer, 0, j + nj)),
                  pl.BlockSpec((tm, tn), lambda i, j: (i, j))],
        out_specs=pl.BlockSpec((tm, tn), lambda i, j: (i, j)),
        out_shape=jax.ShapeDtypeStruct((n, d), F32),
        compiler_params=_cparams(("parallel", "arbitrary"), 40),
        name="glu_residual",
    )(g, w, w, x)


def _mm_res_kernel(a_ref, w_ref, x_ref, o_ref):
    o_ref[...] = x_ref[...] + _dot(a_ref[...], w_ref[...])


def matmul_residual(a, w, layer, x):
    n, k = a.shape
    d = w.shape[2]
    tm, tn = min(ROW_TILE, n), min(COL_TILE, d)
    return pl.pallas_call(
        _mm_res_kernel,
        grid=(n // tm, d // tn),
        in_specs=[pl.BlockSpec((tm, k), lambda i, j: (i, 0)),
                  pl.BlockSpec((None, k, tn), lambda i, j: (layer, 0, j)),
                  pl.BlockSpec((tm, tn), lambda i, j: (i, j))],
        out_specs=pl.BlockSpec((tm, tn), lambda i, j: (i, j)),
        out_shape=jax.ShapeDtypeStruct((n, d), F32),
        compiler_params=_cparams(("parallel", "arbitrary"), 40),
        name="matmul_residual",
    )(a, w, x)


def _ffn_kernel(x_ref, gn_ref, wg_ref, wu_ref, wo_ref, o_ref, xn_ref):
    @pl.when(pl.program_id(1) == 0)
    def _():
        x = x_ref[...]
        xn_ref[...] = _rmsnorm(x, gn_ref[...]).astype(BF16)
        o_ref[...] = x

    xn = xn_ref[...]
    hid = (jax.nn.silu(_dot(xn, wg_ref[...])) * _dot(xn, wu_ref[...])).astype(BF16)
    o_ref[...] += _dot(hid, wo_ref[...])


def ffn_residual(x, gn, w_in, w_out, layer):
    n, d = x.shape
    f = w_out.shape[1]
    tm = min(FFN_ROW_TILE, n)
    tf = COL_TILE if f % COL_TILE == 0 else 256
    nf = f // tf
    return pl.pallas_call(
        _ffn_kernel,
        grid=(n // tm, nf),
        in_specs=[pl.BlockSpec((tm, d), lambda i, j: (i, 0), pipeline_mode=pl.Buffered(1)),
                  pl.BlockSpec((1, d), lambda i, j: (0, 0)),
                  pl.BlockSpec((None, d, tf), lambda i, j: (layer, 0, j)),
                  pl.BlockSpec((None, d, tf), lambda i, j: (layer, 0, j + nf)),
                  pl.BlockSpec((None, tf, d), lambda i, j: (layer, j, 0))],
        out_specs=pl.BlockSpec((tm, d), lambda i, j: (i, 0)),
        out_shape=jax.ShapeDtypeStruct((n, d), F32),
        scratch_shapes=[pltpu.VMEM((tm, d), BF16)],
        compiler_params=_cparams(("parallel", "arbitrary"), 56),
        name="ffn_residual",
    )(x, gn.reshape(1, d), w_in, w_in, w_out)


def _proj_kernel(x_ref, gn_ref, w_ref, hg_ref, *refs, n_norm_blocks, head_dim, scale):
    out_refs, xn_ref = refs[:-1], refs[-1]
    j = pl.program_id(1)

    @pl.when(j == 0)
    def _():
        xn_ref[...] = _rmsnorm(x_ref[...], gn_ref[...]).astype(BF16)

    acc = _dot(xn_ref[...], w_ref[...])
    tn = acc.shape[1]

    @pl.when(j < n_norm_blocks)
    def _():
        for c in range(tn // head_dim):
            cols = slice(c * head_dim, (c + 1) * head_dim)
            y = _rmsnorm(acc[:, cols], hg_ref[...])
            if scale != 1.0:
                y = y * scale
            for o_ref in out_refs:
                o_ref[:, cols] = y.astype(o_ref.dtype)

    @pl.when(j >= n_norm_blocks)
    def _():
        for o_ref in out_refs:
            o_ref[...] = acc.astype(o_ref.dtype)


def norm_proj_headnorm(x, gn, w, layer, col0, m, hg, n_norm_cols, scale, out_dtypes):
    n, d = x.shape
    hd = hg.shape[0]
    tm, tn = min(ROW_TILE, n), min(PROJ_COL_TILE, m)
    assert n_norm_cols % tn == 0 and tn % hd == 0 and col0 % tn == 0
    j0 = col0 // tn
    out_spec = pl.BlockSpec((tm, tn), lambda i, j: (i, j))
    return pl.pallas_call(
        functools.partial(_proj_kernel, n_norm_blocks=n_norm_cols // tn, head_dim=hd, scale=scale),
        grid=(n // tm, m // tn),
        in_specs=[pl.BlockSpec((tm, d), lambda i, j: (i, 0)),
                  pl.BlockSpec((1, d), lambda i, j: (0, 0)),
                  pl.BlockSpec((None, d, tn), lambda i, j: (layer, 0, j0 + j)),
                  pl.BlockSpec((1, hd), lambda i, j: (0, 0))],
        out_specs=[out_spec] * len(out_dtypes),
        out_shape=[jax.ShapeDtypeStruct((n, m), dt) for dt in out_dtypes],
        scratch_shapes=[pltpu.VMEM((tm, d), BF16)],
        compiler_params=_cparams(("parallel", "arbitrary"), 48),
        name="norm_proj_headnorm",
    )(x, gn.reshape(1, d), w, hg.reshape(1, hd))


def _bucket_thresholds():
    half = NUM_BUCKETS // 2
    max_exact = half // 2
    steps = half - max_exact
    thr = []
    for j in range(1, steps):
        n = max_exact
        while (n ** steps) * (max_exact ** j) < (max_exact ** steps) * (MAX_DISTANCE ** j):
            n += 1
        thr.append(n)
    return max_exact, thr


def _bias_kernel(rb_ref, o_ref, *, n_hm, q0, k_start, k_stride, n_keys):
    hm = pl.program_id(0)
    tq, tk = o_ref.shape
    k0 = k_start + pl.program_id(1) * k_stride
    col = lax.broadcasted_iota(jnp.int32, (tq, tk), 1)
    qpos = q0 + lax.broadcasted_iota(jnp.int32, (tq, tk), 0)
    kpos = k0 + col
    n = qpos - kpos
    na = jnp.abs(n)
    max_exact, thr = _bucket_thresholds()
    large = jnp.full((tq, tk), max_exact, jnp.int32)
    for t in thr:
        large = large + jnp.where(na >= t, 1, 0)
    bucket = jnp.where(n < 0, NUM_BUCKETS // 2, 0) + jnp.where(na < max_exact, na, large)
    val = jnp.zeros((tq, tk), F32)
    for b in range(NUM_BUCKETS):
        val = jnp.where(bucket == b, rb_ref[b * n_hm + hm], val)
    shift = CHUNK.bit_length() - 1
    visible = lax.shift_right_logical(kpos, shift) <= lax.shift_right_logical(qpos, shift)
    if n_keys < tk:
        visible = jnp.logical_and(visible, col < n_keys)
    o_ref[...] = jnp.where(visible, val, NEG_INF)


def bias_tiles(rel_bias, tq, tk, n_tiles, q0, k_start, k_stride, n_keys=None):
    n_hm = rel_bias.shape[1] * rel_bias.shape[2]
    return pl.pallas_call(
        functools.partial(_bias_kernel, n_hm=n_hm, q0=q0, k_start=k_start, k_stride=k_stride,
                          n_keys=tk if n_keys is None else n_keys),
        grid=(n_hm, n_tiles),
        in_specs=[pl.BlockSpec(memory_space=pltpu.SMEM)],
        out_specs=pl.BlockSpec((None, tq, tk), lambda a, t: (t, a, 0)),
        out_shape=jax.ShapeDtypeStruct((n_tiles, n_hm * tq, tk), F32),
        compiler_params=_cparams(("parallel", "parallel"), 32),
        name="rel_bias_tiles",
    )(rel_bias.reshape(-1))


def _diff_lambda(dl_ref, lam_init):
    dl = dl_ref[...]
    return (jnp.exp(jnp.sum(dl[0:1] * dl[1:2], axis=-1, keepdims=True))
            - jnp.exp(jnp.sum(dl[2:3] * dl[3:4], axis=-1, keepdims=True)) + lam_init)


def _diff_combine(acc, l, lam, sg, lam_init, rows):
    o = acc[:rows] / l[:rows] - lam * (acc[rows:] / l[rows:])
    return _rmsnorm(o, sg) * (1.0 - lam_init)


def _lane_tile(x, width):
    return jnp.concatenate([x] * (width // LANES), axis=1)


def _softmax_update_wide(s, v, m_ref, l_ref, acc_ref, shift=None):
    w = s.shape[1]
    m_prev = m_ref[...]
    part = s[:, :LANES]
    for c in range(1, w // LANES):
        part = jnp.maximum(part, s[:, c * LANES:(c + 1) * LANES])
    row_max = jnp.max(part, axis=-1, keepdims=True)
    if shift is None:
        m_new = jnp.maximum(m_prev, row_max)
        base = m_new
    else:
        m_new = jnp.maximum(m_prev, row_max + shift)
        base = m_new - shift
    alpha = jnp.exp(m_prev - m_new)
    p = jnp.exp(s - _lane_tile(base, w))
    psum = p[:, :LANES]
    for c in range(1, w // LANES):
        psum = psum + p[:, c * LANES:(c + 1) * LANES]
    l_ref[...] = alpha * l_ref[...] + psum
    pb = p.astype(BF16)
    if isinstance(v, (list, tuple)):
        rb = pb.shape[0] // len(v)
        pv = jnp.concatenate([_dot(pb[n * rb:(n + 1) * rb], vn) for n, vn in enumerate(v)], axis=0)
    else:
        pv = _dot(pb, v)
    acc_ref[...] = _lane_tile(alpha, acc_ref.shape[1]) * acc_ref[...] + pv
    m_ref[...] = m_new


PROMPT_FAR_SPANS = (4, 2, 1)


def _attn_prompt_kernel(dl_ref, sg_ref, q_ref, k_ref, v_ref, b_ref, o_ref, m_sc, l_sc, acc_sc,
                        *, head_dim, lam_init):
    i = pl.program_id(2)
    t = q_ref.shape[0]
    m_sc[...] = jnp.full(m_sc.shape, NEG_INF, F32)
    l_sc[...] = jnp.zeros(l_sc.shape, F32)
    acc_sc[...] = jnp.zeros(acc_sc.shape, F32)
    q0 = q_ref[:, :head_dim]
    q1 = q_ref[:, head_dim:]

    def logits(first_tile, n_tiles):
        rows = pl.ds(pl.multiple_of(first_tile * t, t), n_tiles * t)
        k = k_ref[rows, :]
        s = jnp.concatenate([_dot_nt(q0, k[:, :head_dim]), _dot_nt(q1, k[:, head_dim:])], axis=0)
        return s, v_ref[rows, :]

    far_shift = b_ref[2, :, 0:LANES]
    n_far = jnp.maximum(i - 1, 0)
    done = 0
    for span in PROMPT_FAR_SPANS:
        count = (n_far - done) // span

        def far_body(g, carry, span=span, done=done):
            s, v = logits(done + g * span, span)
            _softmax_update_wide(s, v, m_sc, l_sc, acc_sc, shift=far_shift)
            return carry

        lax.fori_loop(0, count, far_body, 0)
        done = done + count * span

    @pl.when(i >= 1)
    def _():
        s, v = logits(i - 1, 2)
        bias = jnp.concatenate([b_ref[1], b_ref[0]], axis=1)
        _softmax_update_wide(s + bias, v, m_sc, l_sc, acc_sc)

    @pl.when(i == 0)
    def _():
        s, v = logits(0, 1)
        _softmax_update_wide(s + b_ref[0], v, m_sc, l_sc, acc_sc)

    lam = _diff_lambda(dl_ref, lam_init)
    l = jnp.sum(l_sc[...], axis=-1, keepdims=True)
    o_ref[...] = _diff_combine(acc_sc[...], l, lam, sg_ref[...], lam_init, t).astype(o_ref.dtype)


def attn_prompt(q, k, v, bias, dl, sg, n_heads, head_dim, lam_init):
    bsz, seq, width = q.shape
    t = bias.shape[-1]
    hw = 2 * head_dim
    return pl.pallas_call(
        functools.partial(_attn_prompt_kernel, head_dim=head_dim, lam_init=lam_init),
        grid=(bsz, n_heads, seq // t),
        in_specs=[pl.BlockSpec((4, head_dim), lambda b, h, i: (0, 0)),
                  pl.BlockSpec((1, hw), lambda b, h, i: (0, 0)),
                  pl.BlockSpec((None, t, hw), lambda b, h, i: (b, i, h)),
                  pl.BlockSpec((None, seq, hw), lambda b, h, i: (b, 0, h)),
                  pl.BlockSpec((None, seq, hw), lambda b, h, i: (b, 0, h)),
                  pl.BlockSpec((3, 2 * t, t), lambda b, h, i: (0, h, 0))],
        out_specs=pl.BlockSpec((None, t, hw), lambda b, h, i: (b, i, h)),
        out_shape=jax.ShapeDtypeStruct((bsz, seq, width), BF16),
        scratch_shapes=[pltpu.VMEM((2 * t, LANES), F32), pltpu.VMEM((2 * t, LANES), F32),
                        pltpu.VMEM((2 * t, hw), F32)],
        compiler_params=_cparams(("parallel", "parallel", "arbitrary"), 40),
        name="attn_prompt",
    )(dl, sg.reshape(1, hw), q, k, v, bias)


def _attn_decode_kernel(dl_ref, sg_ref, q_ref, kn_ref, vn_ref, bn_ref, bc_ref, kc_hbm, vc_hbm, o_ref,
                        kbuf, vbuf, sem, m_sc, l_sc, acc_sc, *, n_heads, head_dim, lam_init):
    b, j = pl.program_id(0), pl.program_id(1)
    nk = pl.num_programs(1)
    n_steps = pl.num_programs(0) * nk
    step = b * nk + j
    slot = lax.rem(step, 2)
    tk = kbuf.shape[2]
    lq = q_ref.shape[0]
    hw = 2 * head_dim

    def tile_copies(bb, jj, sl):
        rows = pl.ds(pl.multiple_of(jj * tk, tk), tk)
        cps = [pltpu.make_async_copy(kc_hbm.at[bb, rows, hm, :], kbuf.at[sl, hm], sem.at[sl])
               for hm in range(2 * n_heads)]
        cps += [pltpu.make_async_copy(vc_hbm.at[bb, rows, h, :], vbuf.at[sl, h], sem.at[sl])
                for h in range(n_heads)]
        return cps

    @pl.when(step == 0)
    def _():
        for cp in tile_copies(b, j, slot):
            cp.start()

    @pl.when(step + 1 < n_steps)
    def _():
        wrap = j + 1 == nk
        for cp in tile_copies(jnp.where(wrap, b + 1, b), jnp.where(wrap, 0, j + 1), 1 - slot):
            cp.start()

    def attend(k_of, v_of, bias):
        s = jnp.concatenate([_dot_nt(q_ref[:, hm * head_dim:(hm + 1) * head_dim], k_of(hm))
                             for hm in range(2 * n_heads)], axis=0) + bias
        _softmax_update_wide(s, [v_of(h) for h in range(n_heads)], m_sc, l_sc, acc_sc)

    @pl.when(j == 0)
    def _():
        m_sc[...] = jnp.full(m_sc.shape, NEG_INF, F32)
        l_sc[...] = jnp.zeros(l_sc.shape, F32)
        acc_sc[...] = jnp.zeros(acc_sc.shape, F32)
        pad = jnp.zeros((bn_ref.shape[1] - lq, kn_ref.shape[1]), BF16)
        kn = jnp.concatenate([kn_ref[...].astype(BF16), pad], axis=0)
        vn = jnp.concatenate([vn_ref[...].astype(BF16), pad], axis=0)
        attend(lambda hm: kn[:, hm * head_dim:(hm + 1) * head_dim],
               lambda h: vn[:, h * hw:(h + 1) * hw], bn_ref[...])

    for cp in tile_copies(b, j, slot):
        cp.wait()
    attend(lambda hm: kbuf[slot, hm].astype(BF16), lambda h: vbuf[slot, h].astype(BF16), bc_ref[...])

    @pl.when(j == nk - 1)
    def _():
        lam = _diff_lambda(dl_ref, lam_init)
        l = jnp.sum(l_sc[...], axis=-1, keepdims=True)
        for h in range(n_heads):
            rows = slice(2 * h * lq, 2 * (h + 1) * lq)
            o_ref[:, h * hw:(h + 1) * hw] = _diff_combine(
                acc_sc[rows, :], l[rows], lam, sg_ref[...], lam_init, lq).astype(o_ref.dtype)


def attn_decode(q, k_new, v_new, cache_k, cache_v, bias_new, bias_cache, dl, sg, n_heads, head_dim, lam_init):
    bsz, lq, width = q.shape
    past = cache_k.shape[1]
    tk = bias_cache.shape[-1]
    hw = 2 * head_dim
    rows = 2 * n_heads * lq
    return pl.pallas_call(
        functools.partial(_attn_decode_kernel, n_heads=n_heads, head_dim=head_dim, lam_init=lam_init),
        grid=(bsz, past // tk),
        in_specs=[pl.BlockSpec((4, head_dim), lambda b, j: (0, 0)),
                  pl.BlockSpec((1, hw), lambda b, j: (0, 0)),
                  pl.BlockSpec((None, lq, width), lambda b, j: (b, 0, 0)),
                  pl.BlockSpec((None, lq, width), lambda b, j: (b, 0, 0)),
                  pl.BlockSpec((None, lq, width), lambda b, j: (b, 0, 0)),
                  pl.BlockSpec((None, rows, bias_new.shape[-1]), lambda b, j: (0, 0, 0)),
                  pl.BlockSpec((None, rows, tk), lambda b, j: (j, 0, 0)),
                  pl.BlockSpec(memory_space=pl.ANY),
                  pl.BlockSpec(memory_space=pl.ANY)],
        out_specs=pl.BlockSpec((None, lq, width), lambda b, j: (b, 0, 0)),
        out_shape=jax.ShapeDtypeStruct((bsz, lq, width), BF16),
        scratch_shapes=[pltpu.VMEM((2, 2 * n_heads, tk, head_dim), F32),
                        pltpu.VMEM((2, n_heads, tk, hw), F32),
                        pltpu.SemaphoreType.DMA((2,)),
                        pltpu.VMEM((rows, LANES), F32), pltpu.VMEM((rows, LANES), F32),
                        pltpu.VMEM((rows, hw), F32)],
        compiler_params=_cparams(("arbitrary", "arbitrary"), 48),
        name="attn_decode",
    )(dl, sg.reshape(1, hw), q, k_new, v_new, bias_new, bias_cache, cache_k, cache_v)


def _ssm_params(lam_re, lam_im, log_dt, b_re, b_im, c_re, c_im, d):
    g, p, hg = b_re.shape
    dt = jnp.exp(log_dt)[:, None]
    mag = jnp.exp(lam_re * dt)
    ang = lam_im * dt
    a_re = mag * jnp.cos(ang)
    a_im = mag * jnp.sin(ang)
    den = lam_re * lam_re + lam_im * lam_im
    n_re = a_re - 1.0
    n_im = a_im
    f_re = (n_re * lam_re + n_im * lam_im) / den
    f_im = (n_im * lam_re - n_re * lam_im) / den
    bb_re = f_re[..., None] * b_re - f_im[..., None] * b_im
    bb_im = f_re[..., None] * b_im + f_im[..., None] * b_re
    gb = min(SSM_GROUPS_PER_BLOCK, g)
    nb = g // gb
    same = (jnp.arange(gb * hg)[:, None] // hg) == (jnp.arange(gb * p)[None, :] // p)

    def in_proj(bb):
        t = bb.transpose(0, 2, 1).reshape(nb, gb * hg, p)
        return jnp.where(same, jnp.tile(t, (1, 1, gb)), 0.0).astype(BF16)

    def out_proj(c):
        t = c.reshape(nb, gb * hg, p).transpose(0, 2, 1)
        return jnp.where(same.T, jnp.tile(t, (1, gb, 1)), 0.0).astype(BF16)

    return dict(a_re=a_re.reshape(1, g * p), a_im=a_im.reshape(1, g * p),
                bbr=in_proj(bb_re), bbi=in_proj(bb_im),
                ccr=out_proj(c_re), cci=out_proj(-c_im), d=d.reshape(1, g * hg))


def kernel(x_prompt, x_sample, state_ssm_re, state_ssm_im, cache_k, cache_v, mix_norm_g, ffn_norm_g, ssm_lambda_re, ssm_lambda_im, ssm_log_dt, ssm_b_re, ssm_b_im, ssm_c_re, ssm_c_im, ssm_d, w_glu, kv_norm_g, w_kv, k_norm_g, w_q, q_norm_g, diff_lambda, subln_g, w_o, rel_bias, w_ffn_in, w_ffn_out):
    bp, seq, d = x_prompt.shape
    bs, dseq, _ = x_sample.shape
    depth = mix_norm_g.shape[0]
    n_a = ssm_lambda_re.shape[0]
    n_groups, n_state = ssm_lambda_re.shape[1:]
    past, n_heads, _, head_dim = cache_k.shape[1:]
    qk_width = n_heads * 2 * head_dim
    streams = SSM_STREAMS
    n_seg = streams // bp
    seg_len = seq // n_seg
    assert bs == streams and bp * n_seg == streams and seg_len % SSM_TIME_CHUNK == 0
    assert dseq % SSM_TIME_CHUNK == 0 and seq % ATTN_TILE == 0 and past % DEC_KEY_TILE == 0
    assert ATTN_TILE % CHUNK == 0 and ATTN_TILE >= MAX_DISTANCE

    hp = x_prompt.reshape(bp, n_seg, seg_len, d).transpose(2, 0, 1, 3).reshape(seg_len * streams, d)
    hs = x_sample.transpose(1, 0, 2).reshape(dseq * bs, d)
    zeros = jnp.zeros((streams, n_groups * n_state), F32)

    wg, wq, wo = w_glu.astype(BF16), w_q.astype(BF16), w_o.astype(BF16)
    w_in, w_out = w_ffn_in.astype(BF16), w_ffn_out.astype(BF16)

    ssm_re_p, ssm_im_p, ssm_re_s, ssm_im_s = [], [], [], []
    k_p = v_p = k_s = v_s = None
    bias_p = bias_sn = bias_sc = None
    for layer in range(depth):
        gm = mix_norm_g[layer]
        if layer < n_a:
            prm = _ssm_params(ssm_lambda_re[layer], ssm_lambda_im[layer], ssm_log_dt[layer],
                              ssm_b_re[layer], ssm_b_im[layer], ssm_c_re[layer], ssm_c_im[layer],
                              ssm_d[layer])
            up = rmsnorm_rows(hp, gm)
            _, er, ei = ssm_scan(up, zeros, zeros, prm, emit_y=False)
            sr, si = seg_combine(er, ei, prm["a_re"], prm["a_im"], bp, n_seg, seg_len)
            gp, er, ei = ssm_scan(up, sr, si, prm, emit_y=True)
            last = er.reshape(bp, n_seg, n_groups, n_state)[:, -1], ei.reshape(bp, n_seg, n_groups, n_state)[:, -1]
            ssm_re_p.append(last[0])
            ssm_im_p.append(last[1])
            hp = glu_residual(gp, wg, layer, hp)
            us = rmsnorm_rows(hs, gm)
            gs, er, ei = ssm_scan(us, state_ssm_re[layer].reshape(bs, -1), state_ssm_im[layer].reshape(bs, -1),
                                  prm, emit_y=True)
            ssm_re_s.append(er.reshape(bs, n_groups, n_state))
            ssm_im_s.append(ei.reshape(bs, n_groups, n_state))
            hs = glu_residual(gs, wg, layer, hs)
        else:
            jb = layer - n_a
            lam_init = 0.8 - 0.6 * math.exp(-0.3 * layer)
            if layer == n_a:
                hp = hp.reshape(seg_len, bp, n_seg, d).transpose(1, 2, 0, 3).reshape(bp * seq, d)
                hs = hs.reshape(dseq, bs, d).transpose(1, 0, 2).reshape(bs * dseq, d)
                wkv = w_kv.astype(BF16)[None]
                v_width = w_kv.shape[1] - qk_width
                k_p, kb_p = norm_proj_headnorm(hp, kv_norm_g, wkv, 0, 0, qk_width, k_norm_g, qk_width, 1.0, (F32, BF16))
                v_p, vb_p = norm_proj_headnorm(hp, kv_norm_g, wkv, 0, qk_width, v_width, k_norm_g, 0, 1.0, (F32, BF16))
                k_s, = norm_proj_headnorm(hs, kv_norm_g, wkv, 0, 0, qk_width, k_norm_g, qk_width, 1.0, (F32,))
                v_s, = norm_proj_headnorm(hs, kv_norm_g, wkv, 0, qk_width, v_width, k_norm_g, 0, 1.0, (F32,))
                t = ATTN_TILE
                bias_p = bias_tiles(rel_bias, t, t, 3, 2 * t, 2 * t, -t)
                bias_sn = bias_tiles(rel_bias, dseq, -(-dseq // LANES) * LANES, 1, past, past, 0, n_keys=dseq)
                bias_sc = bias_tiles(rel_bias, dseq, DEC_KEY_TILE, past // DEC_KEY_TILE, past, 0, DEC_KEY_TILE)
            scale = head_dim ** -0.5
            qp, = norm_proj_headnorm(hp, gm, wq, jb, 0, qk_width, q_norm_g[jb], qk_width, scale, (BF16,))
            op = attn_prompt(qp.reshape(bp, seq, qk_width), kb_p.reshape(bp, seq, qk_width),
                             vb_p.reshape(bp, seq, -1), bias_p, diff_lambda[jb], subln_g[jb],
                             n_heads, head_dim, lam_init)
            hp = matmul_residual(op.reshape(bp * seq, -1), wo, jb, hp)
            qs, = norm_proj_headnorm(hs, gm, wq, jb, 0, qk_width, q_norm_g[jb], qk_width, scale, (BF16,))
            os_ = attn_decode(qs.reshape(bs, dseq, qk_width), k_s.reshape(bs, dseq, qk_width),
                              v_s.reshape(bs, dseq, -1), cache_k.reshape(bs, past, 2 * n_heads, head_dim),
                              cache_v, bias_sn, bias_sc, diff_lambda[jb], subln_g[jb],
                              n_heads, head_dim, lam_init)
            hs = matmul_residual(os_.reshape(bs * dseq, -1), wo, jb, hs)
        hp = ffn_residual(hp, ffn_norm_g[layer], w_in, w_out, layer)
        hs = ffn_residual(hs, ffn_norm_g[layer], w_in, w_out, layer)

    return (hp.reshape(bp, seq, d), hs.reshape(bs, dseq, d),
            jnp.stack(ssm_re_p), jnp.stack(ssm_im_p),
            k_p.reshape(bp, seq, n_heads, 2, head_dim), v_p.reshape(bp, seq, n_heads, 2 * head_dim),
            jnp.stack(ssm_re_s), jnp.stack(ssm_im_s),
            k_s.reshape(bs, dseq, n_heads, 2, head_dim), v_s.reshape(bs, dseq, n_heads, 2 * head_dim))
```

```python
import functools
import math

import jax
import jax.numpy as jnp
from jax import lax
from jax.experimental import pallas as pl
from jax.experimental.pallas import tpu as pltpu

F32 = jnp.float32
BF16 = jnp.bfloat16

EPS = 1e-6
NEG_INF = -1e30
CHUNK = 64
NUM_BUCKETS = 32
MAX_DISTANCE = 128

V7X_VMEM_BYTES = 64 * 1024 * 1024
LANES = 128
SUBLANES = 8

SSM_GROUPS_PER_BLOCK = 16
SSM_TIME_CHUNK = 16
SSM_STREAMS = 32
SSM_SCAN_LANES = 256
ROW_TILE = 512
COL_TILE = 512
PROJ_COL_TILE = 1024
FFN_ROW_TILE = 1024
ATTN_TILE = 256
DEC_KEY_TILE = 512


def _cparams(semantics, vmem_mb):
    return pltpu.CompilerParams(dimension_semantics=semantics,
                                vmem_limit_bytes=min(vmem_mb * 1024 * 1024, V7X_VMEM_BYTES))


def _rmsnorm(x, g):
    ms = jnp.mean(x * x, axis=-1, keepdims=True)
    return x * lax.rsqrt(ms + EPS) * g


def _dot(a, b):
    return jnp.dot(a, b, preferred_element_type=F32)


def _dot_nt(a, b):
    return lax.dot_general(a, b, (((1,), (1,)), ((), ())), preferred_element_type=F32)


def _norm_kernel(x_ref, g_ref, o_ref):
    o_ref[...] = _rmsnorm(x_ref[...], g_ref[...]).astype(o_ref.dtype)


def rmsnorm_rows(x, g, out_dtype=F32):
    n, d = x.shape
    tm = min(ROW_TILE, n)
    return pl.pallas_call(
        _norm_kernel,
        grid=(n // tm,),
        in_specs=[pl.BlockSpec((tm, d), lambda i: (i, 0)),
                  pl.BlockSpec((1, d), lambda i: (0, 0))],
        out_specs=pl.BlockSpec((tm, d), lambda i: (i, 0)),
        out_shape=jax.ShapeDtypeStruct((n, d), out_dtype),
        compiler_params=_cparams(("parallel",), 32),
        name="rmsnorm_rows",
    )(x, g.reshape(1, d))


def _ssm_kernel(*refs, streams, steps, emit_y):
    if emit_y:
        (u_ref, h0r_ref, h0i_ref, ar_ref, ai_ref, bbr_ref, bbi_ref, ccr_ref, cci_ref, d_ref,
         g_ref, her_ref, hei_ref) = refs
    else:
        (u_ref, h0r_ref, h0i_ref, ar_ref, ai_ref, bbr_ref, bbi_ref, her_ref, hei_ref) = refs

    @pl.when(pl.program_id(1) == 0)
    def _():
        her_ref[...] = h0r_ref[...]
        hei_ref[...] = h0i_ref[...]

    u = u_ref[...]
    ub = u.astype(BF16)
    n_chan, n_state = bbr_ref.shape
    width = SSM_SCAN_LANES
    chan_per_chunk = width * n_chan // n_state
    assert LANES % chan_per_chunk == 0
    ys = [None] * (n_chan // LANES)
    for c in range(n_state // width):
        cols = slice(c * width, (c + 1) * width)
        slab = (c * chan_per_chunk) // LANES
        chans = slice(slab * LANES, (slab + 1) * LANES)
        bu_re = _dot(ub[:, chans], bbr_ref[chans, cols])
        bu_im = _dot(ub[:, chans], bbi_ref[chans, cols])
        ar = jnp.broadcast_to(ar_ref[:, cols], (streams, width))
        ai = jnp.broadcast_to(ai_ref[:, cols], (streams, width))
        hr = her_ref[:, cols]
        hi = hei_ref[:, cols]
        hrs, his = [], []
        for t in range(steps):
            rows = slice(t * streams, (t + 1) * streams)
            hr, hi = ar * hr - ai * hi + bu_re[rows], ar * hi + ai * hr + bu_im[rows]
            hrs.append(hr)
            his.append(hi)
        her_ref[:, cols] = hr
        hei_ref[:, cols] = hi
        if emit_y:
            part = (_dot(jnp.concatenate(hrs, axis=0).astype(BF16), ccr_ref[cols, chans])
                    + _dot(jnp.concatenate(his, axis=0).astype(BF16), cci_ref[cols, chans]))
            ys[slab] = part if ys[slab] is None else ys[slab] + part

    if emit_y:
        y = jnp.concatenate(ys, axis=1) + d_ref[...] * u
        g_ref[...] = jax.nn.gelu(y).astype(g_ref.dtype)


def ssm_scan(u, h0r, h0i, prm, emit_y):
    n, d = u.shape
    s = h0r.shape[0]
    steps = SSM_TIME_CHUNK
    rows = steps * s
    nb, cb, sb = prm["bbr"].shape
    grid = (nb, n // rows)
    in_specs = [
        pl.BlockSpec((rows, cb), lambda b, t: (t, b)),
        pl.BlockSpec((s, sb), lambda b, t: (0, b)),
        pl.BlockSpec((s, sb), lambda b, t: (0, b)),
        pl.BlockSpec((1, sb), lambda b, t: (0, b)),
        pl.BlockSpec((1, sb), lambda b, t: (0, b)),
        pl.BlockSpec((None, cb, sb), lambda b, t: (b, 0, 0)),
        pl.BlockSpec((None, cb, sb), lambda b, t: (b, 0, 0)),
    ]
    args = [u, h0r, h0i, prm["a_re"], prm["a_im"], prm["bbr"], prm["bbi"]]
    state_spec = pl.BlockSpec((s, sb), lambda b, t: (0, b))
    state_shape = jax.ShapeDtypeStruct(h0r.shape, F32)
    if emit_y:
        in_specs += [
            pl.BlockSpec((None, sb, cb), lambda b, t: (b, 0, 0)),
            pl.BlockSpec((None, sb, cb), lambda b, t: (b, 0, 0)),
            pl.BlockSpec((1, cb), lambda b, t: (0, b)),
        ]
        args += [prm["ccr"], prm["cci"], prm["d"]]
        out_specs = [pl.BlockSpec((rows, cb), lambda b, t: (t, b)), state_spec, state_spec]
        out_shape = [jax.ShapeDtypeStruct((n, d), BF16), state_shape, state_shape]
    else:
        out_specs = [state_spec, state_spec]
        out_shape = [state_shape, state_shape]
    outs = pl.pallas_call(
        functools.partial(_ssm_kernel, streams=s, steps=steps, emit_y=emit_y),
        grid=grid,
        in_specs=in_specs,
        out_specs=out_specs,
        out_shape=out_shape,
        compiler_params=_cparams(("parallel", "arbitrary"), 40),
        name="ssm_scan_y" if emit_y else "ssm_scan_state",
    )(*args)
    if emit_y:
        return outs[0], outs[1], outs[2]
    return None, outs[0], outs[1]


def _seg_combine_kernel(er_ref, ei_ref, ar_ref, ai_ref, hr_ref, hi_ref, *, n_batch, n_seg, seg_len):
    pr, pi = ar_ref[...], ai_ref[...]
    tr, ti = None, None
    e = seg_len
    while e:
        if e & 1:
            if tr is None:
                tr, ti = pr, pi
            else:
                tr, ti = tr * pr - ti * pi, tr * pi + ti * pr
        e >>= 1
        if e:
            pr, pi = pr * pr - pi * pi, 2.0 * pr * pi
    for b in range(n_batch):
        hr = jnp.zeros_like(tr)
        hi = jnp.zeros_like(ti)
        for s in range(n_seg):
            row = slice(b * n_seg + s, b * n_seg + s + 1)
            hr_ref[row, :] = hr
            hi_ref[row, :] = hi
            hr, hi = tr * hr - ti * hi + er_ref[row, :], tr * hi + ti * hr + ei_ref[row, :]


def seg_combine(er, ei, a_re, a_im, n_batch, n_seg, seg_len):
    s, n = er.shape
    tn = min(1024, n)
    spec = pl.BlockSpec((s, tn), lambda j: (0, j))
    aspec = pl.BlockSpec((1, tn), lambda j: (0, j))
    return pl.pallas_call(
        functools.partial(_seg_combine_kernel, n_batch=n_batch, n_seg=n_seg, seg_len=seg_len),
        grid=(n // tn,),
        in_specs=[spec, spec, aspec, aspec],
        out_specs=[spec, spec],
        out_shape=[jax.ShapeDtypeStruct((s, n), F32)] * 2,
        compiler_params=_cparams(("parallel",), 32),
        name="ssm_seg_combine",
    )(er, ei, a_re, a_im)


def _glu_kernel(g_ref, wa_ref, wb_ref, x_ref, o_ref):
    g = g_ref[...]
    o_ref[...] = x_ref[...] + _dot(g, wa_ref[...]) * jax.nn.sigmoid(_dot(g, wb_ref[...]))


def glu_residual(g, w, layer, x):
    n, k = g.shape
    d = x.shape[1]
    tm, tn = min(ROW_TILE, n), min(COL_TILE, d)
    nj = d // tn
    return pl.pallas_call(
        _glu_kernel,
        grid=(n // tm, nj),
        in_specs=[pl.BlockSpec((tm, k), lambda i, j: (i, 0)),
                  pl.BlockSpec((None, k, tn), lambda i, j: (layer, 0, j)),
                  pl.BlockSpec((None, k, tn), lambda i, j: (layer, 0, j + nj)),
                  pl.BlockSpec((tm, tn), lambda i, j: (i, j))],
        out_specs=pl.BlockSpec((tm, tn), lambda i, j: (i, j)),
        out_shape=jax.ShapeDtypeStruct((n, d), F32),
        compiler_params=_cparams(("parallel", "arbitrary"), 40),
        name="glu_residual",
    )(g, w, w, x)


def _mm_res_kernel(a_ref, w_ref, x_ref, o_ref):
    o_ref[...] = x_ref[...] + _dot(a_ref[...], w_ref[...])


def matmul_residual(a, w, layer, x):
    n, k = a.shape
    d = w.shape[2]
    tm, tn = min(ROW_TILE, n), min(COL_TILE, d)
    return pl.pallas_call(
        _mm_res_kernel,
        grid=(n // tm, d // tn),
        in_specs=[pl.BlockSpec((tm, k), lambda i, j: (i, 0)),
                  pl.BlockSpec((None, k, tn), lambda i, j: (layer, 0, j)),
                  pl.BlockSpec((tm, tn), lambda i, j: (i, j))],
        out_specs=pl.BlockSpec((tm, tn), lambda i, j: (i, j)),
        out_shape=jax.ShapeDtypeStruct((n, d), F32),
        compiler_params=_cparams(("parallel", "arbitrary"), 40),
        name="matmul_residual",
    )(a, w, x)


def _ffn_kernel(x_ref, gn_ref, wg_ref, wu_ref, wo_ref, o_ref, xn_ref):
    @pl.when(pl.program_id(1) == 0)
    def _():
        x = x_ref[...]
        xn_ref[...] = _rmsnorm(x, gn_ref[...]).astype(BF16)
        o_ref[...] = x

    xn = xn_ref[...]
    hid = (jax.nn.silu(_dot(xn, wg_ref[...])) * _dot(xn, wu_ref[...])).astype(BF16)
    o_ref[...] += _dot(hid, wo_ref[...])


def ffn_residual(x, gn, w_in, w_out, layer):
    n, d = x.shape
    f = w_out.shape[1]
    tm = min(FFN_ROW_TILE, n)
    tf = COL_TILE if f % COL_TILE == 0 else 256
    nf = f // tf
    return pl.pallas_call(
        _ffn_kernel,
        grid=(n // tm, nf),
        in_specs=[pl.BlockSpec((tm, d), lambda i, j: (i, 0)),
                  pl.BlockSpec((1, d), lambda i, j: (0, 0)),
                  pl.BlockSpec((None, d, tf), lambda i, j: (layer, 0, j)),
                  pl.BlockSpec((None, d, tf), lambda i, j: (layer, 0, j + nf)),
                  pl.BlockSpec((None, tf, d), lambda i, j: (layer, j, 0))],
        out_specs=pl.BlockSpec((tm, d), lambda i, j: (i, 0)),
        out_shape=jax.ShapeDtypeStruct((n, d), F32),
        scratch_shapes=[pltpu.VMEM((tm, d), BF16)],
        compiler_params=_cparams(("parallel", "arbitrary"), 56),
        name="ffn_residual",
    )(x, gn.reshape(1, d), w_in, w_in, w_out)


def _proj_kernel(x_ref, gn_ref, w_ref, hg_ref, *refs, n_norm_blocks, head_dim, scale):
    out_refs, xn_ref = refs[:-1], refs[-1]
    j = pl.program_id(1)

    @pl.when(j == 0)
    def _():
        xn_ref[...] = _rmsnorm(x_ref[...], gn_ref[...]).astype(BF16)

    acc = _dot(xn_ref[...], w_ref[...])
    tn = acc.shape[1]

    @pl.when(j < n_norm_blocks)
    def _():
        for c in range(tn // head_dim):
            cols = slice(c * head_dim, (c + 1) * head_dim)
            y = _rmsnorm(acc[:, cols], hg_ref[...])
            if scale != 1.0:
                y = y * scale
            for o_ref in out_refs:
                o_ref[:, cols] = y.astype(o_ref.dtype)

    @pl.when(j >= n_norm_blocks)
    def _():
        for o_ref in out_refs:
            o_ref[...] = acc.astype(o_ref.dtype)


def norm_proj_headnorm(x, gn, w, layer, col0, m, hg, n_norm_cols, scale, out_dtypes):
    n, d = x.shape
    hd = hg.shape[0]
    tm, tn = min(ROW_TILE, n), min(PROJ_COL_TILE, m)
    assert n_norm_cols % tn == 0 and tn % hd == 0 and col0 % tn == 0
    j0 = col0 // tn
    out_spec = pl.BlockSpec((tm, tn), lambda i, j: (i, j))
    return pl.pallas_call(
        functools.partial(_proj_kernel, n_norm_blocks=n_norm_cols // tn, head_dim=hd, scale=scale),
        grid=(n // tm, m // tn),
        in_specs=[pl.BlockSpec((tm, d), lambda i, j: (i, 0)),
                  pl.BlockSpec((1, d), lambda i, j: (0, 0)),
                  pl.BlockSpec((None, d, tn), lambda i, j: (layer, 0, j0 + j)),
                  pl.BlockSpec((1, hd), lambda i, j: (0, 0))],
        out_specs=[out_spec] * len(out_dtypes),
        out_shape=[jax.ShapeDtypeStruct((n, m), dt) for dt in out_dtypes],
        scratch_shapes=[pltpu.VMEM((tm, d), BF16)],
        compiler_params=_cparams(("parallel", "arbitrary"), 48),
        name="norm_proj_headnorm",
    )(x, gn.reshape(1, d), w, hg.reshape(1, hd))


def _bucket_thresholds():
    half = NUM_BUCKETS // 2
    max_exact = half // 2
    steps = half - max_exact
    thr = []
    for j in range(1, steps):
        n = max_exact
        while (n ** steps) * (max_exact ** j) < (max_exact ** steps) * (MAX_DISTANCE ** j):
            n += 1
        thr.append(n)
    return max_exact, thr


def _bias_kernel(rb_ref, o_ref, *, n_hm, q0, k_start, k_stride, n_keys):
    hm = pl.program_id(0)
    tq, tk = o_ref.shape
    k0 = k_start + pl.program_id(1) * k_stride
    col = lax.broadcasted_iota(jnp.int32, (tq, tk), 1)
    qpos = q0 + lax.broadcasted_iota(jnp.int32, (tq, tk), 0)
    kpos = k0 + col
    n = qpos - kpos
    na = jnp.abs(n)
    max_exact, thr = _bucket_thresholds()
    large = jnp.full((tq, tk), max_exact, jnp.int32)
    for t in thr:
        large = large + jnp.where(na >= t, 1, 0)
    bucket = jnp.where(n < 0, NUM_BUCKETS // 2, 0) + jnp.where(na < max_exact, na, large)
    val = jnp.zeros((tq, tk), F32)
    for b in range(NUM_BUCKETS):
        val = jnp.where(bucket == b, rb_ref[b * n_hm + hm], val)
    shift = CHUNK.bit_length() - 1
    visible = lax.shift_right_logical(kpos, shift) <= lax.shift_right_logical(qpos, shift)
    if n_keys < tk:
        visible = jnp.logical_and(visible, col < n_keys)
    o_ref[...] = jnp.where(visible, val, NEG_INF)


def bias_tiles(rel_bias, tq, tk, n_tiles, q0, k_start, k_stride, n_keys=None):
    n_hm = rel_bias.shape[1] * rel_bias.shape[2]
    return pl.pallas_call(
        functools.partial(_bias_kernel, n_hm=n_hm, q0=q0, k_start=k_start, k_stride=k_stride,
                          n_keys=tk if n_keys is None else n_keys),
        grid=(n_hm, n_tiles),
        in_specs=[pl.BlockSpec(memory_space=pltpu.SMEM)],
        out_specs=pl.BlockSpec((None, tq, tk), lambda a, t: (t, a, 0)),
        out_shape=jax.ShapeDtypeStruct((n_tiles, n_hm * tq, tk), F32),
        compiler_params=_cparams(("parallel", "parallel"), 32),
        name="rel_bias_tiles",
    )(rel_bias.reshape(-1))


def _diff_lambda(dl_ref, lam_init):
    dl = dl_ref[...]
    return (jnp.exp(jnp.sum(dl[0:1] * dl[1:2], axis=-1, keepdims=True))
            - jnp.exp(jnp.sum(dl[2:3] * dl[3:4], axis=-1, keepdims=True)) + lam_init)


def _diff_combine(acc, l, lam, sg, lam_init, rows):
    o = acc[:rows] / l[:rows] - lam * (acc[rows:] / l[rows:])
    return _rmsnorm(o, sg) * (1.0 - lam_init)


def _lane_tile(x, width):
    return jnp.concatenate([x] * (width // LANES), axis=1)


def _softmax_update_wide(s, v, m_ref, l_ref, acc_ref, shift=None):
    w = s.shape[1]
    m_prev = m_ref[...]
    part = s[:, :LANES]
    for c in range(1, w // LANES):
        part = jnp.maximum(part, s[:, c * LANES:(c + 1) * LANES])
    row_max = jnp.max(part, axis=-1, keepdims=True)
    if shift is None:
        m_new = jnp.maximum(m_prev, row_max)
        base = m_new
    else:
        m_new = jnp.maximum(m_prev, row_max + shift)
        base = m_new - shift
    alpha = jnp.exp(m_prev - m_new)
    p = jnp.exp(s - _lane_tile(base, w))
    psum = p[:, :LANES]
    for c in range(1, w // LANES):
        psum = psum + p[:, c * LANES:(c + 1) * LANES]
    l_ref[...] = alpha * l_ref[...] + psum
    pb = p.astype(BF16)
    if isinstance(v, (list, tuple)):
        rb = pb.shape[0] // len(v)
        pv = jnp.concatenate([_dot(pb[n * rb:(n + 1) * rb], vn) for n, vn in enumerate(v)], axis=0)
    else:
        pv = _dot(pb, v)
    acc_ref[...] = _lane_tile(alpha, acc_ref.shape[1]) * acc_ref[...] + pv
    m_ref[...] = m_new


PROMPT_FAR_SPANS = (4, 2, 1)


def _attn_prompt_kernel(dl_ref, sg_ref, q_ref, k_ref, v_ref, b_ref, o_ref, m_sc, l_sc, acc_sc,
                        *, head_dim, lam_init):
    i = pl.program_id(2)
    t = q_ref.shape[0]
    m_sc[...] = jnp.full(m_sc.shape, NEG_INF, F32)
    l_sc[...] = jnp.zeros(l_sc.shape, F32)
    acc_sc[...] = jnp.zeros(acc_sc.shape, F32)
    q0 = q_ref[:, :head_dim]
    q1 = q_ref[:, head_dim:]

    def logits(first_tile, n_tiles):
        rows = pl.ds(pl.multiple_of(first_tile * t, t), n_tiles * t)
        k = k_ref[rows, :]
        s = jnp.concatenate([_dot_nt(q0, k[:, :head_dim]), _dot_nt(q1, k[:, head_dim:])], axis=0)
        return s, v_ref[rows, :]

    far_shift = b_ref[2, :, 0:LANES]
    n_far = jnp.maximum(i - 1, 0)
    done = 0
    for span in PROMPT_FAR_SPANS:
        count = (n_far - done) // span

        def far_body(g, carry, span=span, done=done):
            s, v = logits(done + g * span, span)
            _softmax_update_wide(s, v, m_sc, l_sc, acc_sc, shift=far_shift)
            return carry

        lax.fori_loop(0, count, far_body, 0)
        done = done + count * span

    @pl.when(i >= 1)
    def _():
        s, v = logits(i - 1, 2)
        bias = jnp.concatenate([b_ref[1], b_ref[0]], axis=1)
        _softmax_update_wide(s + bias, v, m_sc, l_sc, acc_sc)

    @pl.when(i == 0)
    def _():
        s, v = logits(0, 1)
        _softmax_update_wide(s + b_ref[0], v, m_sc, l_sc, acc_sc)

    lam = _diff_lambda(dl_ref, lam_init)
    l = jnp.sum(l_sc[...], axis=-1, keepdims=True)
    o_ref[...] = _diff_combine(acc_sc[...], l, lam, sg_ref[...], lam_init, t).astype(o_ref.dtype)


def attn_prompt(q, k, v, bias, dl, sg, n_heads, head_dim, lam_init):
    bsz, seq, width = q.shape
    t = bias.shape[-1]
    hw = 2 * head_dim
    return pl.pallas_call(
        functools.partial(_attn_prompt_kernel, head_dim=head_dim, lam_init=lam_init),
        grid=(bsz, n_heads, seq // t),
        in_specs=[pl.BlockSpec((4, head_dim), lambda b, h, i: (0, 0)),
                  pl.BlockSpec((1, hw), lambda b, h, i: (0, 0)),
                  pl.BlockSpec((None, t, hw), lambda b, h, i: (b, i, h)),
                  pl.BlockSpec((None, seq, hw), lambda b, h, i: (b, 0, h)),
                  pl.BlockSpec((None, seq, hw), lambda b, h, i: (b, 0, h)),
                  pl.BlockSpec((3, 2 * t, t), lambda b, h, i: (0, h, 0))],
        out_specs=pl.BlockSpec((None, t, hw), lambda b, h, i: (b, i, h)),
        out_shape=jax.ShapeDtypeStruct((bsz, seq, width), BF16),
        scratch_shapes=[pltpu.VMEM((2 * t, LANES), F32), pltpu.VMEM((2 * t, LANES), F32),
                        pltpu.VMEM((2 * t, hw), F32)],
        compiler_params=_cparams(("parallel", "parallel", "arbitrary"), 40),
        name="attn_prompt",
    )(dl, sg.reshape(1, hw), q, k, v, bias)


def _attn_decode_kernel(dl_ref, sg_ref, q_ref, kn_ref, vn_ref, bn_ref, bc_ref, kc_hbm, vc_hbm, o_ref,
                        kbuf, vbuf, sem, m_sc, l_sc, acc_sc, *, n_heads, head_dim, lam_init):
    b, j = pl.program_id(0), pl.program_id(1)
    nk = pl.num_programs(1)
    n_steps = pl.num_programs(0) * nk
    step = b * nk + j
    slot = lax.rem(step, 2)
    tk = kbuf.shape[2]
    lq = q_ref.shape[0]
    hw = 2 * head_dim

    def tile_copies(bb, jj, sl):
        rows = pl.ds(pl.multiple_of(jj * tk, tk), tk)
        cps = [pltpu.make_async_copy(kc_hbm.at[bb, rows, hm, :], kbuf.at[sl, hm], sem.at[sl])
               for hm in range(2 * n_heads)]
        cps += [pltpu.make_async_copy(vc_hbm.at[bb, rows, h, :], vbuf.at[sl, h], sem.at[sl])
                for h in range(n_heads)]
        return cps

    @pl.when(step == 0)
    def _():
        for cp in tile_copies(b, j, slot):
            cp.start()

    @pl.when(step + 1 < n_steps)
    def _():
        wrap = j + 1 == nk
        for cp in tile_copies(jnp.where(wrap, b + 1, b), jnp.where(wrap, 0, j + 1), 1 - slot):
            cp.start()

    def attend(k_of, v_of, bias):
        s = jnp.concatenate([_dot_nt(q_ref[:, hm * head_dim:(hm + 1) * head_dim], k_of(hm))
                             for hm in range(2 * n_heads)], axis=0) + bias
        _softmax_update_wide(s, [v_of(h) for h in range(n_heads)], m_sc, l_sc, acc_sc)

    @pl.when(j == 0)
    def _():
        m_sc[...] = jnp.full(m_sc.shape, NEG_INF, F32)
        l_sc[...] = jnp.zeros(l_sc.shape, F32)
        acc_sc[...] = jnp.zeros(acc_sc.shape, F32)
        pad = jnp.zeros((bn_ref.shape[1] - lq, kn_ref.shape[1]), BF16)
        kn = jnp.concatenate([kn_ref[...].astype(BF16), pad], axis=0)
        vn = jnp.concatenate([vn_ref[...].astype(BF16), pad], axis=0)
        attend(lambda hm: kn[:, hm * head_dim:(hm + 1) * head_dim],
               lambda h: vn[:, h * hw:(h + 1) * hw], bn_ref[...])

    for cp in tile_copies(b, j, slot):
        cp.wait()
    attend(lambda hm: kbuf[slot, hm].astype(BF16), lambda h: vbuf[slot, h].astype(BF16), bc_ref[...])

    @pl.when(j == nk - 1)
    def _():
        lam = _diff_lambda(dl_ref, lam_init)
        l = jnp.sum(l_sc[...], axis=-1, keepdims=True)
        for h in range(n_heads):
            rows = slice(2 * h * lq, 2 * (h + 1) * lq)
            o_ref[:, h * hw:(h + 1) * hw] = _diff_combine(
                acc_sc[rows, :], l[rows], lam, sg_ref[...], lam_init, lq).astype(o_ref.dtype)


def attn_decode(q, k_new, v_new, cache_k, cache_v, bias_new, bias_cache, dl, sg, n_heads, head_dim, lam_init):
    bsz, lq, width = q.shape
    past = cache_k.shape[1]
    tk = bias_cache.shape[-1]
    hw = 2 * head_dim
    rows = 2 * n_heads * lq
    return pl.pallas_call(
        functools.partial(_attn_decode_kernel, n_heads=n_heads, head_dim=head_dim, lam_init=lam_init),
        grid=(bsz, past // tk),
        in_specs=[pl.BlockSpec((4, head_dim), lambda b, j: (0, 0)),
                  pl.BlockSpec((1, hw), lambda b, j: (0, 0)),
                  pl.BlockSpec((None, lq, width), lambda b, j: (b, 0, 0)),
                  pl.BlockSpec((None, lq, width), lambda b, j: (b, 0, 0)),
                  pl.BlockSpec((None, lq, width), lambda b, j: (b, 0, 0)),
                  pl.BlockSpec((None, rows, bias_new.shape[-1]), lambda b, j: (0, 0, 0)),
                  pl.BlockSpec((None, rows, tk), lambda b, j: (j, 0, 0)),
                  pl.BlockSpec(memory_space=pl.ANY),
                  pl.BlockSpec(memory_space=pl.ANY)],
        out_specs=pl.BlockSpec((None, lq, width), lambda b, j: (b, 0, 0)),
        out_shape=jax.ShapeDtypeStruct((bsz, lq, width), BF16),
        scratch_shapes=[pltpu.VMEM((2, 2 * n_heads, tk, head_dim), F32),
                        pltpu.VMEM((2, n_heads, tk, hw), F32),
                        pltpu.SemaphoreType.DMA((2,)),
                        pltpu.VMEM((rows, LANES), F32), pltpu.VMEM((rows, LANES), F32),
                        pltpu.VMEM((rows, hw), F32)],
        compiler_params=_cparams(("arbitrary", "arbitrary"), 48),
        name="attn_decode",
    )(dl, sg.reshape(1, hw), q, k_new, v_new, bias_new, bias_cache, cache_k, cache_v)


def _ssm_params(lam_re, lam_im, log_dt, b_re, b_im, c_re, c_im, d):
    g, p, hg = b_re.shape
    dt = jnp.exp(log_dt)[:, None]
    mag = jnp.exp(lam_re * dt)
    ang = lam_im * dt
    a_re = mag * jnp.cos(ang)
    a_im = mag * jnp.sin(ang)
    den = lam_re * lam_re + lam_im * lam_im
    n_re = a_re - 1.0
    n_im = a_im
    f_re = (n_re * lam_re + n_im * lam_im) / den
    f_im = (n_im * lam_re - n_re * lam_im) / den
    bb_re = f_re[..., None] * b_re - f_im[..., None] * b_im
    bb_im = f_re[..., None] * b_im + f_im[..., None] * b_re
    gb = min(SSM_GROUPS_PER_BLOCK, g)
    nb = g // gb
    same = (jnp.arange(gb * hg)[:, None] // hg) == (jnp.arange(gb * p)[None, :] // p)

    def in_proj(bb):
        t = bb.transpose(0, 2, 1).reshape(nb, gb * hg, p)
        return jnp.where(same, jnp.tile(t, (1, 1, gb)), 0.0).astype(BF16)

    def out_proj(c):
        t = c.reshape(nb, gb * hg, p).transpose(0, 2, 1)
        return jnp.where(same.T, jnp.tile(t, (1, gb, 1)), 0.0).astype(BF16)

    return dict(a_re=a_re.reshape(1, g * p), a_im=a_im.reshape(1, g * p),
                bbr=in_proj(bb_re), bbi=in_proj(bb_im),
                ccr=out_proj(c_re), cci=out_proj(-c_im), d=d.reshape(1, g * hg))


def kernel(x_prompt, x_sample, state_ssm_re, state_ssm_im, cache_k, cache_v, mix_norm_g, ffn_norm_g, ssm_lambda_re, ssm_lambda_im, ssm_log_dt, ssm_b_re, ssm_b_im, ssm_c_re, ssm_c_im, ssm_d, w_glu, kv_norm_g, w_kv, k_norm_g, w_q, q_norm_g, diff_lambda, subln_g, w_o, rel_bias, w_ffn_in, w_ffn_out):
    bp, seq, d = x_prompt.shape
    bs, dseq, _ = x_sample.shape
    depth = mix_norm_g.shape[0]
    n_a = ssm_lambda_re.shape[0]
    n_groups, n_state = ssm_lambda_re.shape[1:]
    past, n_heads, _, head_dim = cache_k.shape[1:]
    qk_width = n_heads * 2 * head_dim
    streams = SSM_STREAMS
    n_seg = streams // bp
    seg_len = seq // n_seg
    assert bs == streams and bp * n_seg == streams and seg_len % SSM_TIME_CHUNK == 0
    assert dseq % SSM_TIME_CHUNK == 0 and seq % ATTN_TILE == 0 and past % DEC_KEY_TILE == 0
    assert ATTN_TILE % CHUNK == 0 and ATTN_TILE >= MAX_DISTANCE

    hp = x_prompt.reshape(bp, n_seg, seg_len, d).transpose(2, 0, 1, 3).reshape(seg_len * streams, d)
    hs = x_sample.transpose(1, 0, 2).reshape(dseq * bs, d)
    zeros = jnp.zeros((streams, n_groups * n_state), F32)

    wg, wq, wo = w_glu.astype(BF16), w_q.astype(BF16), w_o.astype(BF16)
    w_in, w_out = w_ffn_in.astype(BF16), w_ffn_out.astype(BF16)

    ssm_re_p, ssm_im_p, ssm_re_s, ssm_im_s = [], [], [], []
    k_p = v_p = k_s = v_s = None
    bias_p = bias_sn = bias_sc = None
    for layer in range(depth):
        gm = mix_norm_g[layer]
        if layer < n_a:
            prm = _ssm_params(ssm_lambda_re[layer], ssm_lambda_im[layer], ssm_log_dt[layer],
                              ssm_b_re[layer], ssm_b_im[layer], ssm_c_re[layer], ssm_c_im[layer],
                              ssm_d[layer])
            up = rmsnorm_rows(hp, gm)
            _, er, ei = ssm_scan(up, zeros, zeros, prm, emit_y=False)
            sr, si = seg_combine(er, ei, prm["a_re"], prm["a_im"], bp, n_seg, seg_len)
            gp, er, ei = ssm_scan(up, sr, si, prm, emit_y=True)
            last = er.reshape(bp, n_seg, n_groups, n_state)[:, -1], ei.reshape(bp, n_seg, n_groups, n_state)[:, -1]
            ssm_re_p.append(last[0])
            ssm_im_p.append(last[1])
            hp = glu_residual(gp, wg, layer, hp)
            us = rmsnorm_rows(hs, gm)
            gs, er, ei = ssm_scan(us, state_ssm_re[layer].reshape(bs, -1), state_ssm_im[layer].reshape(bs, -1),
                                  prm, emit_y=True)
            ssm_re_s.append(er.reshape(bs, n_groups, n_state))
            ssm_im_s.append(ei.reshape(bs, n_groups, n_state))
            hs = glu_residual(gs, wg, layer, hs)
        else:
            jb = layer - n_a
            lam_init = 0.8 - 0.6 * math.exp(-0.3 * layer)
            if layer == n_a:
                hp = hp.reshape(seg_len, bp, n_seg, d).transpose(1, 2, 0, 3).reshape(bp * seq, d)
                hs = hs.reshape(dseq, bs, d).transpose(1, 0, 2).reshape(bs * dseq, d)
                wkv = w_kv.astype(BF16)[None]
                v_width = w_kv.shape[1] - qk_width
                k_p, kb_p = norm_proj_headnorm(hp, kv_norm_g, wkv, 0, 0, qk_width, k_norm_g, qk_width, 1.0, (F32, BF16))
                v_p, vb_p = norm_proj_headnorm(hp, kv_norm_g, wkv, 0, qk_width, v_width, k_norm_g, 0, 1.0, (F32, BF16))
                k_s, = norm_proj_headnorm(hs, kv_norm_g, wkv, 0, 0, qk_width, k_norm_g, qk_width, 1.0, (F32,))
                v_s, = norm_proj_headnorm(hs, kv_norm_g, wkv, 0, qk_width, v_width, k_norm_g, 0, 1.0, (F32,))
                t = ATTN_TILE
                bias_p = bias_tiles(rel_bias, t, t, 3, 2 * t, 2 * t, -t)
                bias_sn = bias_tiles(rel_bias, dseq, -(-dseq // LANES) * LANES, 1, past, past, 0, n_keys=dseq)
                bias_sc = bias_tiles(rel_bias, dseq, DEC_KEY_TILE, past // DEC_KEY_TILE, past, 0, DEC_KEY_TILE)
            scale = head_dim ** -0.5
            qp, = norm_proj_headnorm(hp, gm, wq, jb, 0, qk_width, q_norm_g[jb], qk_width, scale, (BF16,))
            op = attn_prompt(qp.reshape(bp, seq, qk_width), kb_p.reshape(bp, seq, qk_width),
                             vb_p.reshape(bp, seq, -1), bias_p, diff_lambda[jb], subln_g[jb],
                             n_heads, head_dim, lam_init)
            hp = matmul_residual(op.reshape(bp * seq, -1), wo, jb, hp)
            qs, = norm_proj_headnorm(hs, gm, wq, jb, 0, qk_width, q_norm_g[jb], qk_width, scale, (BF16,))
            os_ = attn_decode(qs.reshape(bs, dseq, qk_width), k_s.reshape(bs, dseq, qk_width),
                              v_s.reshape(bs, dseq, -1), cache_k.reshape(bs, past, 2 * n_heads, head_dim),
                              cache_v, bias_sn, bias_sc, diff_lambda[jb], subln_g[jb],
                              n_heads, head_dim, lam_init)
            hs = matmul_residual(os_.reshape(bs * dseq, -1), wo, jb, hs)
        hp = ffn_residual(hp, ffn_norm_g[layer], w_in, w_out, layer)
        hs = ffn_residual(hs, ffn_norm_g[layer], w_in, w_out, layer)

    return (hp.reshape(bp, seq, d), hs.reshape(bs, dseq, d),
            jnp.stack(ssm_re_p), jnp.stack(ssm_im_p),
            k_p.reshape(bp, seq, n_heads, 2, head_dim), v_p.reshape(bp, seq, n_heads, 2 * head_dim),
            jnp.stack(ssm_re_s), jnp.stack(ssm_im_s),
            k_s.reshape(bs, dseq, n_heads, 2, head_dim), v_s.reshape(bs, dseq, n_heads, 2 * head_dim))
```
